```python
import math
import jax, jax.numpy as jnp
from jax import lax
import numpy as np

D_MODEL = 1024
BATCH = 16
SEQ = 2048
DEPTH = 4

CTX_LEN = 256
GRID_W = 64
EPS = 1e-6
BRANCH_W = 512
N_BRANCH = 3
CONV_K = 4
DK_A = 128
H_A = BRANCH_W // DK_A
W_A = H_A * DK_A
CHUNK = 64
W_B = BRANCH_W
NB_B = 8
BW_B = W_B // NB_B
RGLRU_C = 8.0
DH_C = 128
H_C = BRANCH_W // DH_C
W_C = H_C * DH_C
WIN_R = 8
WIN_C = 16
QCB = 16
KCB = 32
ROPE_BASE = 10000.0
SPLIT_SIZES = (3 * W_A, W_A, 2 * H_A, 2 * H_A, W_B, W_B, 3 * W_C, W_C, N_BRANCH * D_MODEL)
N_IN = sum(SPLIT_SIZES)

kernel_name = "hybrid_gdn_rglru_natten_prefix_trunk"


def rms_norm(x, g):
    xf = x.astype(jnp.float32)
    y = xf * lax.rsqrt(jnp.mean(xf * xf, axis=-1, keepdims=True) + EPS)
    return (y * g.astype(jnp.float32)).astype(x.dtype)


def l2_normalize(x):
    xf = x.astype(jnp.float32)
    return (xf * lax.rsqrt(jnp.sum(xf * xf, axis=-1, keepdims=True) + EPS)).astype(x.dtype)


def same(t):
    return t


def flip_seq(t):
    return jnp.flip(t, axis=1)


def conv_centred(x, w):
    k = w.shape[0]
    return lax.conv_general_dilated(
        x, w[:, None, :], window_strides=(1,), padding=[(k // 2, k - 1 - k // 2)],
        dimension_numbers=("NWC", "WIO", "NWC"), feature_group_count=x.shape[-1])


def axial_rope(x, rows, cols):
    half = x.shape[-1] // 2
    quarter = half // 2
    inv_freq = ROPE_BASE ** (-jnp.arange(quarter, dtype=jnp.float32) / quarter)

    def rotate(xa, pos):
        ang = pos.astype(jnp.float32)[:, None] * inv_freq
        cos = jnp.cos(ang)[None, :, None, :].astype(x.dtype)
        sin = jnp.sin(ang)[None, :, None, :].astype(x.dtype)
        x1, x2 = xa[..., :quarter], xa[..., quarter:]
        return jnp.concatenate([x1 * cos - x2 * sin, x2 * cos + x1 * sin], axis=-1)

    return jnp.concatenate([rotate(x[..., :half], rows), rotate(x[..., half:], cols)], axis=-1)


def gated_delta_rule(q, k, v, g, beta, s0):
    out_dtype = v.dtype
    bsz, length, heads, dk = q.shape
    dv = v.shape[-1]
    n = length // CHUNK
    f32 = jnp.float32

    def blocks(t):
        t = t.astype(f32).reshape(bsz, n, CHUNK, heads, *t.shape[3:])
        return jnp.moveaxis(t, (1, 3), (0, 2))

    qb, kb, vb, gb, bb = blocks(q), blocks(k), blocks(v), blocks(g), blocks(beta)
    gc = jnp.cumsum(gb, axis=-1)
    idx = jnp.arange(CHUNK)
    incl = idx[:, None] >= idx[None, :]
    strict = idx[:, None] > idx[None, :]
    diff = gc[..., :, None] - gc[..., None, :]
    decay = jnp.where(incl, jnp.exp(jnp.where(incl, diff, 0.0)), 0.0)
    k_beta = kb * bb[..., None]
    a_mat = jnp.where(strict, jnp.einsum("nbhid,nbhjd->nbhij", k_beta, kb) * decay, 0.0)
    lhs = a_mat + jnp.eye(CHUNK, dtype=f32)
    rhs = jnp.concatenate([vb * bb[..., None], k_beta * jnp.exp(gc)[..., None]], axis=-1)
    sol = lax.linalg.triangular_solve(lhs, rhs, left_side=True, lower=True, unit_diagonal=True)
    w_v, w_k = sol[..., :dv], sol[..., dv:]
    qk = jnp.einsum("nbhid,nbhjd->nbhij", qb, kb) * decay
    q_dec = qb * jnp.exp(gc)[..., None]
    k_tail = kb * jnp.exp(gc[..., -1:] - gc)[..., None]
    g_tail = jnp.exp(gc[..., -1])

    def step(s, xs):
        wv, wk, qk_c, qd, kt, gt = xs
        u = wv - jnp.einsum("bhik,bhkv->bhiv", wk, s)
        o = jnp.einsum("bhik,bhkv->bhiv", qd, s) + jnp.einsum("bhij,bhjv->bhiv", qk_c, u)
        s = s * gt[..., None, None] + jnp.einsum("bhik,bhiv->bhkv", kt, u)
        return s, o

    s_final, o = lax.scan(step, s0.astype(f32), (w_v, w_k, qk, q_dec, k_tail, g_tail))
    o = jnp.moveaxis(o, (0, 2), (1, 3)).reshape(bsz, length, heads, dv)
    return o.astype(out_dtype), s_final


def gdn_bidirectional(lat, ctx):
    ql, kl, vl, gl, bl = lat
    qc, kc, vc, gc, bc = ctx
    s0 = jnp.zeros((ql.shape[0], H_A, DK_A, DK_A), jnp.float32)
    outs_lat, outs_ctx = [], []
    for d in range(2):
        f = flip_seq if d == 1 else same
        oc, sc = gated_delta_rule(f(qc), f(kc), f(vc), f(gc[:, :, d]), f(bc[:, :, d]), s0)
        ol, _ = gated_delta_rule(f(ql), f(kl), f(vl), f(gl[:, :, d]), f(bl[:, :, d]), sc)
        outs_lat.append(f(ol))
        outs_ctx.append(f(oc))
    return outs_lat[0] + outs_lat[1], outs_ctx[0] + outs_ctx[1]


def block_diag_linear(x, w, b):
    xb = x.reshape(*x.shape[:-1], NB_B, BW_B)
    return jnp.einsum("blnd,nde->blne", xb, w).reshape(x.shape) + b


def rglru(x, wa, ba, wx, bx, lam, h0):
    xf = x.astype(jnp.float32)
    r = jax.nn.sigmoid(block_diag_linear(xf, wa, ba))
    i = jax.nn.sigmoid(block_diag_linear(xf, wx, bx))
    log_a = -RGLRU_C * r * jax.nn.softplus(-lam.astype(jnp.float32))
    a = jnp.exp(log_a)
    b = jnp.sqrt(-jnp.expm1(2.0 * log_a)) * (i * xf)
    b = b.at[:, 0].add(a[:, 0] * h0)

    def combine(lhs, rhs):
        a1, b1 = lhs
        a2, b2 = rhs
        return a1 * a2, a2 * b1 + b2

    _, h = lax.associative_scan(combine, (a, b), axis=1)
    return h


def rglru_bidirectional(x_lat, x_ctx, wa, ba, wx, bx, lam):
    h0 = jnp.zeros((x_lat.shape[0], W_B), jnp.float32)
    outs_lat, outs_ctx = [], []
    for d in range(2):
        f = flip_seq if d == 1 else same
        h_ctx = rglru(f(x_ctx), wa[d], ba[d], wx[d], bx[d], lam[d], h0)
        h_lat = rglru(f(x_lat), wa[d], ba[d], wx[d], bx[d], lam[d], h_ctx[:, -1])
        outs_lat.append(f(h_lat))
        outs_ctx.append(f(h_ctx))
    return outs_lat[0] + outs_lat[1], outs_ctx[0] + outs_ctx[1]


def neighbourhood_attention(q, k, v, k_ctx, v_ctx, rpb):
    bsz, s_len, heads, dh = q.shape
    rows = s_len // GRID_W
    wr = min(WIN_R, rows)
    ncb = GRID_W // QCB
    scale = dh ** -0.5
    q_cols = np.arange(GRID_W).reshape(ncb, QCB)
    kc_idx = np.clip(np.arange(ncb) * QCB - WIN_C // 2, 0, GRID_W - KCB)[:, None] + np.arange(KCB)
    win_c0 = np.clip(q_cols - WIN_C // 2, 0, GRID_W - WIN_C)
    col_ok = (kc_idx[:, None, :] >= win_c0[..., None]) & (kc_idx[:, None, :] < win_c0[..., None] + WIN_C)
    dc = np.clip(kc_idx[:, None, :] - q_cols[..., None], -(WIN_C - 1), WIN_C - 1) + WIN_C - 1
    rpb_cols = rpb[:, :, dc].astype(jnp.float32)
    qg = jnp.moveaxis(q.reshape(bsz, rows, ncb, QCB, heads, dh), 1, 0)
    kg = k.reshape(bsz, rows, GRID_W, heads, dh)
    vg = v.reshape(bsz, rows, GRID_W, heads, dh)
    n_loc = wr * KCB

    def row_block(args):
        r, qr = args
        r0 = jnp.clip(r - WIN_R // 2, 0, rows - wr)
        kr = lax.dynamic_slice_in_dim(kg, r0, wr, axis=1)[:, :, kc_idx]
        vr = lax.dynamic_slice_in_dim(vg, r0, wr, axis=1)[:, :, kc_idx]
        s_loc = jnp.einsum("bnqhd,bwnkhd->bhnqwk", qr, kr).astype(jnp.float32) * scale
        dr = r0 + jnp.arange(wr) - r + WIN_R - 1
        bias = jnp.transpose(jnp.take(rpb_cols, dr, axis=1), (0, 2, 3, 1, 4))
        s_loc = jnp.where(col_ok[:, :, None, :], s_loc + bias, -jnp.inf)
        s_loc = s_loc.reshape(bsz, heads, ncb, QCB, n_loc)
        s_ctx = jnp.einsum("bnqhd,bchd->bhnqc", qr, k_ctx).astype(jnp.float32) * scale
        p = jax.nn.softmax(jnp.concatenate([s_loc, s_ctx], axis=-1), axis=-1).astype(v.dtype)
        p_loc = p[..., :n_loc].reshape(bsz, heads, ncb, QCB, wr, KCB)
        return (jnp.einsum("bhnqwk,bwnkhd->bnqhd", p_loc, vr)
                + jnp.einsum("bhnqc,bchd->bnqhd", p[..., n_loc:], v_ctx))

    o = lax.map(row_block, (jnp.arange(rows), qg))
    return jnp.moveaxis(o, 0, 1).reshape(bsz, s_len, heads, dh)


def context_attention(q, k, v):
    s = jnp.einsum("bqhd,bkhd->bhqk", q, k).astype(jnp.float32) * (q.shape[-1] ** -0.5)
    p = jax.nn.softmax(s, axis=-1).astype(v.dtype)
    return jnp.einsum("bhqk,bkhd->bqhd", p, v)


def hybrid_mixer(h_lat, h_ctx, rows_idx, cols_idx, w_in, conv_a, a_log, dt_bias, onorm_a,
                 conv_b, conv_b_bias, lru_wa, lru_ba, lru_wx, lru_bx, lru_lam, rpb,
                 w_branch, w_out, with_ctx_out):
    split_at = tuple(int(i) for i in np.cumsum(SPLIT_SIZES)[:-1])
    p_lat = jnp.split(h_lat @ w_in, split_at, axis=-1)
    p_ctx = jnp.split(h_ctx @ w_in, split_at, axis=-1)

    def gdn_inputs(p, rotate):
        qkv, beta_logit, alpha_logit = p[0], p[2], p[3]
        bsz, length, _ = qkv.shape
        qkv = jax.nn.silu(conv_centred(qkv, conv_a)).reshape(bsz, length, 3, H_A, DK_A)
        q, k, v = l2_normalize(qkv[:, :, 0]), l2_normalize(qkv[:, :, 1]), qkv[:, :, 2]
        if rotate:
            q, k = axial_rope(q, rows_idx, cols_idx), axial_rope(k, rows_idx, cols_idx)
        beta = jax.nn.sigmoid(beta_logit).reshape(bsz, length, 2, H_A)
        g = -jnp.exp(a_log) * jax.nn.softplus(alpha_logit.reshape(bsz, length, 2, H_A) + dt_bias)
        return q * DK_A ** -0.5, k, v, g, beta

    def gdn_out(o, gate):
        bsz, length = o.shape[:2]
        return rms_norm(o, onorm_a).reshape(bsz, length, W_A) * jax.nn.silu(gate)

    oa_lat, oa_ctx = gdn_bidirectional(gdn_inputs(p_lat, True), gdn_inputs(p_ctx, False))

    hb_lat, hb_ctx = rglru_bidirectional(conv_centred(p_lat[4], conv_b) + conv_b_bias,
                                         conv_centred(p_ctx[4], conv_b) + conv_b_bias,
                                         lru_wa, lru_ba, lru_wx, lru_bx, lru_lam)

    def lru_out(h, gate):
        return h.astype(gate.dtype) * jax.nn.silu(gate)

    def split_heads(t):
        bsz, length, _ = t.shape
        return t.reshape(bsz, length, 3, H_C, DH_C)

    qkv_lat, qkv_ctx = split_heads(p_lat[6]), split_heads(p_ctx[6])
    oc_lat = neighbourhood_attention(qkv_lat[:, :, 0], qkv_lat[:, :, 1], qkv_lat[:, :, 2],
                                     qkv_ctx[:, :, 1], qkv_ctx[:, :, 2], rpb)

    def attn_out(o, gate):
        bsz, length = o.shape[:2]
        return o.reshape(bsz, length, W_C) * jax.nn.silu(gate)

    def merge(ya, yb, yc, gate_logits):
        gates = jax.nn.sigmoid(gate_logits.reshape(*gate_logits.shape[:-1], N_BRANCH, D_MODEL))
        merged = (gates[..., 0, :] * (ya @ w_branch[0])
                  + gates[..., 1, :] * (yb @ w_branch[1])
                  + gates[..., 2, :] * (yc @ w_branch[2]))
        return merged @ w_out

    y_lat = merge(gdn_out(oa_lat, p_lat[1]), lru_out(hb_lat, p_lat[5]),
                  attn_out(oc_lat, p_lat[7]), p_lat[8])
    if not with_ctx_out:
        return y_lat, None
    oc_ctx = context_attention(qkv_ctx[:, :, 0], qkv_ctx[:, :, 1], qkv_ctx[:, :, 2])
    y_ctx = merge(gdn_out(oa_ctx, p_ctx[1]), lru_out(hb_ctx, p_ctx[5]),
                  attn_out(oc_ctx, p_ctx[7]), p_ctx[8])
    return y_lat, y_ctx


def setup_inputs(seed: int = 0) -> dict:
    key = jax.random.key(seed)
    ks = jax.random.split(key, 24)
    f32 = jnp.float32

    def nrm(k, shape, s):
        return jax.random.normal(k, shape, f32) * s

    dt = jnp.exp(jax.random.uniform(ks[11], (DEPTH, 2, H_A), f32, math.log(1e-3), math.log(1e-1)))
    lam_a = jax.random.uniform(ks[19], (DEPTH, 2, W_B), f32, 0.9, 0.999) ** (1.0 / RGLRU_C)
    return {
        "x": nrm(ks[0], (BATCH, SEQ, D_MODEL), 1.0),
        "c": nrm(ks[1], (BATCH, D_MODEL), 1.0),
        "ctx": nrm(ks[2], (BATCH, CTX_LEN, D_MODEL), 1.0),
        "c_ctx": nrm(ks[3], (D_MODEL,), 1.0),
        "w_mod": nrm(ks[4], (DEPTH, D_MODEL, 3 * D_MODEL), 0.5 * D_MODEL ** -0.5),
        "b_mod": nrm(ks[5], (DEPTH, 3 * D_MODEL), 0.02),
        "g_pre": 1.0 + nrm(ks[6], (DEPTH, D_MODEL), 0.02),
        "g_post": 1.0 + nrm(ks[7], (DEPTH, D_MODEL), 0.02),
        "w_in": nrm(ks[8], (DEPTH, D_MODEL, N_IN), D_MODEL ** -0.5),
        "conv_a": nrm(ks[9], (DEPTH, CONV_K, 3 * W_A), CONV_K ** -0.5),
        "a_log": jnp.log(jax.random.uniform(ks[10], (DEPTH, 2, H_A), f32, 1.0, 16.0)),
        "dt_bias": dt + jnp.log(-jnp.expm1(-dt)),
        "onorm_a": 1.0 + nrm(ks[12], (DEPTH, DK_A), 0.02),
        "conv_b": nrm(ks[13], (DEPTH, CONV_K, W_B), CONV_K ** -0.5),
        "conv_b_bias": nrm(ks[14], (DEPTH, W_B), 0.02),
        "lru_wa": nrm(ks[15], (DEPTH, 2, NB_B, BW_B, BW_B), BW_B ** -0.5),
        "lru_ba": nrm(ks[16], (DEPTH, 2, W_B), 0.02),
        "lru_wx": nrm(ks[17], (DEPTH, 2, NB_B, BW_B, BW_B), BW_B ** -0.5),
        "lru_bx": nrm(ks[18], (DEPTH, 2, W_B), 0.02),
        "lru_lam": jnp.log(lam_a) - jnp.log1p(-lam_a),
        "rpb": nrm(ks[20], (DEPTH, H_C, 2 * WIN_R - 1, 2 * WIN_C - 1), 0.02),
        "w_branch": nrm(ks[21], (DEPTH, N_BRANCH, BRANCH_W, D_MODEL), BRANCH_W ** -0.5),
        "w_out": nrm(ks[22], (DEPTH, D_MODEL, D_MODEL), D_MODEL ** -0.5),
    }


def reference(x, c, ctx, c_ctx, w_mod, b_mod, g_pre, g_post, w_in, conv_a, a_log, dt_bias,
              onorm_a, conv_b, conv_b_bias, lru_wa, lru_ba, lru_wx, lru_bx, lru_lam, rpb,
              w_branch, w_out):
    s_len = x.shape[1]
    pos = jnp.arange(s_len)
    rows_idx, cols_idx = pos // GRID_W, pos % GRID_W
    silu_c = jax.nn.silu(c)
    silu_cc = jax.nn.silu(c_ctx)
    for l in range(DEPTH):
        last = l == DEPTH - 1
        shift_l, scale_l, gate_l = jnp.split(silu_c @ w_mod[l] + b_mod[l], 3, axis=-1)
        shift_c, scale_c, gate_c = jnp.split(silu_cc @ w_mod[l] + b_mod[l], 3, axis=-1)
        h_lat = rms_norm(x, g_pre[l]) * (1.0 + scale_l[:, None]) + shift_l[:, None]
        h_ctx = rms_norm(ctx, g_pre[l]) * (1.0 + scale_c) + shift_c
        y_lat, y_ctx = hybrid_mixer(h_lat, h_ctx, rows_idx, cols_idx, w_in[l], conv_a[l], a_log[l],
                                    dt_bias[l], onorm_a[l], conv_b[l], conv_b_bias[l], lru_wa[l],
                                    lru_ba[l], lru_wx[l], lru_bx[l], lru_lam[l], rpb[l],
                                    w_branch[l], w_out[l], not last)
        x = x + gate_l[:, None] * rms_norm(y_lat, g_post[l])
        if not last:
            ctx = ctx + gate_c * rms_norm(y_ctx, g_post[l])
    return x
```

```python
import functools
import math

import numpy as np
import jax
import jax.numpy as jnp
from jax import lax
from jax.experimental import pallas as pl
from jax.experimental.pallas import tpu as pltpu

F32 = jnp.float32
BF16 = jnp.bfloat16

EPS = 1e-6
GRID_W = 64
CONV_K = 4
H_A = 4
DK = 128
CHUNK = 64
N_CHAIN = 2 * H_A
W_BR = 512
NB_B = 8
RGLRU_C = 8.0
H_C = 4
WIN_R = 8
WIN_C = 16
ROPE_BASE = 10000.0
Q_ROWS = 4
K_ROWS = 12
NEG = -1e30
LRU_BLK = 64
SUBLANES = 8
VMEM_LIMIT = 56 * 1024 * 1024

OFF_AQKV, OFF_AGATE, OFF_BX, OFF_BGATE, OFF_CQKV, OFF_CGATE, OFF_MG = 0, 1536, 2048, 2560, 3072, 4608, 5120
N_MAIN = 8192
N_SMALL = 128


def _cparams(*sem):
    return pltpu.CompilerParams(dimension_semantics=sem, vmem_limit_bytes=VMEM_LIMIT)


def _dot(a, b):
    return jnp.dot(a, b, preferred_element_type=F32)


def _dot_nt(a, b):
    return lax.dot_general(a, b, (((1,), (1,)), ((), ())), preferred_element_type=F32)


def _silu(x):
    return x * jax.nn.sigmoid(x)


def _softplus(x):
    return jnp.maximum(x, 0.0) + jnp.log1p(jnp.exp(-jnp.abs(x)))


def _mod_kernel(c_ref, w_ref, b_ref, o_ref):
    o_ref[0] = _dot(_silu(c_ref[...]).astype(BF16), w_ref[0].astype(BF16)) + b_ref[0]


def _modulation(cvec, w_mod, b_mod):
    depth, d, n = w_mod.shape
    rows = cvec.shape[0]
    tn = 768
    return pl.pallas_call(
        _mod_kernel,
        grid=(depth, n // tn),
        in_specs=[pl.BlockSpec((rows, d), lambda l, j: (0, 0)),
                  pl.BlockSpec((1, d, tn), lambda l, j: (l, 0, j)),
                  pl.BlockSpec((1, 1, tn), lambda l, j: (l, 0, j))],
        out_specs=pl.BlockSpec((1, rows, tn), lambda l, j: (l, 0, j)),
        out_shape=jax.ShapeDtypeStruct((depth, rows, n), F32),
        compiler_params=_cparams("parallel", "parallel"),
        name="modulation",
    )(cvec, w_mod, b_mod.reshape(depth, 1, n))


def _inproj_kernel(n_ctx, x_ref, ml_ref, mc_ref, g_ref, w_ref, ws_ref, o_ref, os_ref, h_ref):
    d = x_ref.shape[-1]

    @pl.when(pl.program_id(1) == 0)
    def _():
        g = g_ref[...]

        def normed(x, mod):
            y = x * lax.rsqrt(jnp.mean(x * x, axis=-1, keepdims=True) + EPS) * g
            return (y * (1.0 + mod[:, d:2 * d]) + mod[:, :d]).astype(BF16)

        h_ref[:n_ctx] = normed(x_ref[0, :n_ctx], mc_ref[...])
        h_ref[n_ctx:] = normed(x_ref[0, n_ctx:], ml_ref[0])
        os_ref[0] = _dot(h_ref[...], ws_ref[...])

    o_ref[0] = _dot(h_ref[...], w_ref[...])


def _in_proj(xs, mod_lat, mod_ctx, g_pre, w_main, w_small, n_ctx):
    b, l, d = xs.shape
    tn = 512
    return pl.pallas_call(
        functools.partial(_inproj_kernel, n_ctx),
        grid=(b, N_MAIN // tn),
        in_specs=[pl.BlockSpec((1, l, d), lambda i, j: (i, 0, 0)),
                  pl.BlockSpec((1, 1, 3 * d), lambda i, j: (i, 0, 0)),
                  pl.BlockSpec((1, 3 * d), lambda i, j: (0, 0)),
                  pl.BlockSpec((1, d), lambda i, j: (0, 0)),
                  pl.BlockSpec((d, tn), lambda i, j: (0, j)),
                  pl.BlockSpec((d, N_SMALL), lambda i, j: (0, 0))],
        out_specs=[pl.BlockSpec((1, l, tn), lambda i, j: (i, 0, j)),
                   pl.BlockSpec((1, l, N_SMALL), lambda i, j: (i, 0, 0))],
        out_shape=[jax.ShapeDtypeStruct((b, l, N_MAIN), F32),
                   jax.ShapeDtypeStruct((b, l, N_SMALL), F32)],
        scratch_shapes=[pltpu.VMEM((l, d), BF16)],
        compiler_params=_cparams("parallel", "arbitrary"),
        name="in_proj",
    )(xs, mod_lat, mod_ctx, g_pre, w_main, w_small)


def _conv_centred(x, w, n_ctx):
    l = x.shape[0]
    t = lax.broadcasted_iota(jnp.int32, (l, 1), 0)
    tl = jnp.where(t < n_ctx, t, t - n_ctx)
    rem = jnp.where(t < n_ctx, n_ctx - 1 - t, l - 1 - t)
    acc = x * w[CONV_K // 2:CONV_K // 2 + 1]
    for k in range(CONV_K):
        off = k - CONV_K // 2
        if off == 0:
            continue
        shifted = pltpu.roll(x, (-off) % l, 0)
        ok = (tl >= -off) if off < 0 else (rem >= off)
        acc = acc + jnp.where(ok, shifted, 0.0) * w[k:k + 1]
    return acc


def _gdn_prep_kernel(n_ctx, pq_ref, pk_ref, pv_ref, wq_ref, wk_ref, wv_ref, cos_ref, sin_ref,
                     q_ref, k_ref, v_ref):
    lane = lax.broadcasted_iota(jnp.int32, (1, DK), 1)
    first = (lane % (DK // 2)) < (DK // 4)

    def qk_path(p_ref, w_ref, scale):
        x = _silu(_conv_centred(p_ref[0], w_ref[...], n_ctx))
        x = x * lax.rsqrt(jnp.sum(x * x, axis=-1, keepdims=True) + EPS)
        partner = jnp.where(first, pltpu.roll(x, DK - DK // 4, 1), pltpu.roll(x, DK // 4, 1))
        x = x * cos_ref[...] + partner * sin_ref[...]
        if scale != 1.0:
            x = x * scale
        return x.astype(BF16)

    q_ref[0] = qk_path(pq_ref, wq_ref, DK ** -0.5)
    k_ref[0] = qk_path(pk_ref, wk_ref, 1.0)
    v_ref[0] = _silu(_conv_centred(pv_ref[0], wv_ref[...], n_ctx)).astype(BF16)


def _gdn_prep(p, conv_a, cos_t, sin_t, n_ctx):
    b, l, _ = p.shape
    cb = OFF_AQKV // DK

    def pspec(off):
        return pl.BlockSpec((1, l, DK), lambda i, h: (i, 0, cb + off + h))

    def wspec(off):
        return pl.BlockSpec((CONV_K, DK), lambda i, h: (0, off + h))

    tab = pl.BlockSpec((l, DK), lambda i, h: (0, 0))
    out = pl.BlockSpec((1, l, DK), lambda i, h: (i, 0, h))
    shp = jax.ShapeDtypeStruct((b, l, H_A * DK), BF16)
    return pl.pallas_call(
        functools.partial(_gdn_prep_kernel, n_ctx),
        grid=(b, H_A),
        in_specs=[pspec(0), pspec(H_A), pspec(2 * H_A), wspec(0), wspec(H_A), wspec(2 * H_A), tab, tab],
        out_specs=[out, out, out],
        out_shape=[shp, shp, shp],
        compiler_params=_cparams("parallel", "parallel"),
        name="gdn_prep",
    )(p, p, p, conv_a, conv_a, conv_a, cos_t, sin_t)


def _gdn_gate_kernel(ps_ref, alog_ref, dt_ref, o_ref):
    x = ps_ref[0]
    l = x.shape[0]
    lane = lax.broadcasted_iota(jnp.int32, (1, N_SMALL), 1)
    g = -jnp.exp(alog_ref[...]) * _softplus(x + dt_ref[...])
    pos = lax.broadcasted_iota(jnp.int32, (l, 1), 0) % CHUNK
    fwd, bwd = g, g
    s = 1
    while s < CHUNK:
        fwd = fwd + jnp.where(pos >= s, pltpu.roll(fwd, s, 0), 0.0)
        bwd = bwd + jnp.where(pos < CHUNK - s, pltpu.roll(bwd, l - s, 0), 0.0)
        s *= 2
    gc = jnp.where(lane < 2 * H_A + H_A, fwd, bwd)
    o_ref[0] = jnp.where(lane < 2 * H_A, jax.nn.sigmoid(x), gc)


def _gdn_gates(psmall, alog_l, dt_l):
    b, l, n = psmall.shape
    vec = pl.BlockSpec((1, n), lambda i: (0, 0))
    return pl.pallas_call(
        _gdn_gate_kernel,
        grid=(b,),
        in_specs=[pl.BlockSpec((1, l, n), lambda i: (i, 0, 0)), vec, vec],
        out_specs=pl.BlockSpec((1, l, n), lambda i: (i, 0, 0)),
        out_shape=jax.ShapeDtypeStruct((b, l, n), F32),
        compiler_params=_cparams("parallel"),
        name="gdn_gates",
    )(psmall, alog_l, dt_l)


def _split_bf16(x):
    hi = x.astype(BF16)
    return hi, (x - hi.astype(F32)).astype(BF16)


def _dot3(a, b):
    ah, al = _split_bf16(a)
    bh, bl = _split_bf16(b)
    return _dot(ah, bh) + (_dot(ah, bl) + _dot(al, bh))


def _unit_tri_inverse(a):
    n = a.shape[0]
    ii = lax.broadcasted_iota(jnp.int32, (n, n), 0)
    jj = lax.broadcasted_iota(jnp.int32, (n, n), 1)
    p = -a
    t = jnp.where(ii == jj, 1.0, 0.0) + p
    s = 2
    while s < n:
        p = _dot3(p, p)
        t = t + _dot3(t, p)
        s *= 2
    return t


def _gdn_kernel(n_ctx, q_ref, k_ref, v_ref, gs_ref, gr_ref, o_ref, wv_ref, wk_ref, qk_ref, s_ref):
    l = q_ref.shape[1]
    n_chunks = l // CHUNK
    nc_ctx = n_ctx // CHUNK
    ii = lax.broadcasted_iota(jnp.int32, (CHUNK, CHUNK), 0)
    jj = lax.broadcasted_iota(jnp.int32, (CHUNK, CHUNK), 1)

    def chunk_inputs(c, d, h):
        rows = pl.ds(pl.multiple_of(c * CHUNK, CHUNK), CHUNK)
        ch = d * H_A + h
        gs = gs_ref[0, rows, :]
        beta = gs[:, ch:ch + 1]
        gcol = gs[:, N_CHAIN + ch:N_CHAIN + ch + 1]
        grow = gr_ref[0, ch, pl.ds(c, 1), :]
        return rows, beta, gcol, grow

    def prepare(c, carry):
        rows = pl.ds(pl.multiple_of(c * CHUNK, CHUNK), CHUNK)
        for h in range(H_A):
            cols = slice(h * DK, (h + 1) * DK)
            q, k, v = q_ref[0, rows, cols], k_ref[0, rows, cols], v_ref[0, rows, cols]
            kk = _dot_nt(k, k)
            qk = _dot_nt(q, k)
            kf, vf = k.astype(F32), v.astype(F32)
            for d in range(2):
                _, beta, gcol, grow = chunk_inputs(c, d, h)
                incl = (ii >= jj) if d == 0 else (ii <= jj)
                strict = (ii > jj) if d == 0 else (ii < jj)
                decay = jnp.where(incl, jnp.exp(jnp.where(incl, gcol - grow, 0.0)), 0.0)
                a = jnp.where(strict, kk * beta * decay, 0.0)
                t = _unit_tri_inverse(a)
                rhs = jnp.concatenate([vf * beta, kf * (beta * jnp.exp(gcol))], axis=-1)
                sol = _dot3(t, rhs)
                ch = d * H_A + h
                wv_ref[ch, rows, :] = sol[:, :DK].astype(BF16)
                wk_ref[ch, rows, :] = sol[:, DK:].astype(BF16)
                qk_ref[ch, rows, :] = (qk * decay).astype(BF16)
        return carry

    lax.fori_loop(0, n_chunks, prepare, 0)

    s_ref[...] = jnp.zeros_like(s_ref)
    o_ref[...] = jnp.zeros_like(o_ref)

    def recur(i, carry):
        c_bwd = jnp.where(i < nc_ctx, nc_ctx - 1 - i, n_chunks + nc_ctx - 1 - i)
        for d in range(2):
            c = i if d == 0 else c_bwd
            for h in range(H_A):
                cols = slice(h * DK, (h + 1) * DK)
                rows, _, gcol, grow = chunk_inputs(c, d, h)
                ch = d * H_A + h
                g_last = grow[:, CHUNK - 1:CHUNK] if d == 0 else grow[:, 0:1]
                s = s_ref[ch]
                s_b = s.astype(BF16)
                u = wv_ref[ch, rows, :].astype(F32) - _dot(wk_ref[ch, rows, :], s_b)
                u_b = u.astype(BF16)
                q_dec = (q_ref[0, rows, cols].astype(F32) * jnp.exp(gcol)).astype(BF16)
                o = _dot(q_dec, s_b) + _dot(qk_ref[ch, rows, :], u_b)
                k_tail = k_ref[0, rows, cols].astype(F32) * jnp.exp(g_last - gcol)
                s_ref[ch] = s * jnp.exp(g_last) + _dot(k_tail.T.astype(BF16), u_b)
                o_ref[0, rows, cols] += o
        return carry

    lax.fori_loop(0, n_chunks, recur, 0)


def _gdn(qn, kn, vn, gsm, grow, n_ctx):
    b, l, w = qn.shape
    n_chunks = l // CHUNK
    seq = pl.BlockSpec((1, l, w), lambda i: (i, 0, 0))
    return pl.pallas_call(
        functools.partial(_gdn_kernel, n_ctx),
        grid=(b,),
        in_specs=[seq, seq, seq,
                  pl.BlockSpec((1, l, N_SMALL), lambda i: (i, 0, 0)),
                  pl.BlockSpec((1, N_CHAIN, n_chunks, CHUNK), lambda i: (i, 0, 0, 0))],
        out_specs=pl.BlockSpec((1, l, w), lambda i: (i, 0, 0)),
        out_shape=jax.ShapeDtypeStruct((b, l, w), F32),
        scratch_shapes=[pltpu.VMEM((N_CHAIN, l, DK), BF16),
                        pltpu.VMEM((N_CHAIN, l, DK), BF16),
                        pltpu.VMEM((N_CHAIN, l, CHUNK), BF16),
                        pltpu.VMEM((N_CHAIN, DK, DK), F32)],
        compiler_params=_cparams("parallel"),
        name="gdn",
    )(qn, kn, vn, gsm, grow)


def _lru_kernel(n_ctx, x_ref, cw_ref, cb_ref, w_ref, b_ref, lam_ref, o_ref, xc_ref, h_ref):
    l = x_ref.shape[1]
    n_blk = l // LRU_BLK
    nb_ctx = n_ctx // LRU_BLK
    wd = x_ref.shape[2]
    xc_ref[...] = _conv_centred(x_ref[0], cw_ref[...], n_ctx) + cb_ref[...]
    h_ref[...] = jnp.zeros_like(h_ref)
    o_ref[...] = jnp.zeros_like(o_ref)
    sub = lax.broadcasted_iota(jnp.int32, (LRU_BLK, 1), 0) % SUBLANES

    def block(i, carry):
        c_bwd = jnp.where(i < nb_ctx, nb_ctx - 1 - i, n_blk + nb_ctx - 1 - i)
        for d in range(2):
            c = i if d == 0 else c_bwd
            rows = pl.ds(pl.multiple_of(c * LRU_BLK, LRU_BLK), LRU_BLK)
            x = xc_ref[rows, :]
            z = _dot(x.astype(BF16), w_ref[d]) + b_ref[d]
            r = jax.nn.sigmoid(z[:, :wd])
            gate_i = jax.nn.sigmoid(z[:, wd:])
            log_a = (-RGLRU_C) * r * _softplus(-lam_ref[d])
            a = jnp.exp(log_a)
            th = jnp.tanh(log_a)
            bb = jnp.sqrt(-2.0 * th / (1.0 - th)) * (gate_i * x)
            s = 1
            while s < SUBLANES:
                if d == 0:
                    ok = sub >= s
                    a_s, b_s = pltpu.roll(a, s, 0), pltpu.roll(bb, s, 0)
                else:
                    ok = sub < SUBLANES - s
                    a_s, b_s = pltpu.roll(a, LRU_BLK - s, 0), pltpu.roll(bb, LRU_BLK - s, 0)
                bb = bb + a * jnp.where(ok, b_s, 0.0)
                a = a * jnp.where(ok, a_s, 1.0)
                s *= 2
            hc = h_ref[d]
            n_grp = LRU_BLK // SUBLANES
            order = range(n_grp) if d == 0 else range(n_grp - 1, -1, -1)
            outs = [None] * n_grp
            for gi in order:
                sl = slice(gi * SUBLANES, (gi + 1) * SUBLANES)
                hg = bb[sl] + a[sl] * hc
                outs[gi] = hg
                hc = hg[SUBLANES - 1:SUBLANES] if d == 0 else hg[0:1]
            h_ref[d] = hc
            o_ref[0, rows, :] += jnp.concatenate(outs, axis=0)
        return carry

    lax.fori_loop(0, n_blk, block, 0)


def _lru(p, conv_b, conv_b_bias, w_bd, b_bd, lam, n_ctx):
    b, l, _ = p.shape
    return pl.pallas_call(
        functools.partial(_lru_kernel, n_ctx),
        grid=(b,),
        in_specs=[pl.BlockSpec((1, l, W_BR), lambda i: (i, 0, OFF_BX // W_BR)),
                  pl.BlockSpec((CONV_K, W_BR), lambda i: (0, 0)),
                  pl.BlockSpec((1, W_BR), lambda i: (0, 0)),
                  pl.BlockSpec((2, W_BR, 2 * W_BR), lambda i: (0, 0, 0)),
                  pl.BlockSpec((2, 1, 2 * W_BR), lambda i: (0, 0, 0)),
                  pl.BlockSpec((2, 1, W_BR), lambda i: (0, 0, 0))],
        out_specs=pl.BlockSpec((1, l, W_BR), lambda i: (i, 0, 0)),
        out_shape=jax.ShapeDtypeStruct((b, l, W_BR), F32),
        scratch_shapes=[pltpu.VMEM((l, W_BR), F32), pltpu.VMEM((2, 1, W_BR), F32)],
        compiler_params=_cparams("parallel"),
        name="rglru",
    )(p, conv_b, conv_b_bias, w_bd, b_bd, lam)


def _attn_kernel(n_ctx, rows, q_ref, k_ref, v_ref, bias_ref, o_ref):
    scale = DK ** -0.5
    n_blk = rows // Q_ROWS
    nq = Q_ROWS * GRID_W
    nk = K_ROWS * GRID_W
    kc = k_ref[0, :n_ctx, :].astype(BF16)
    vc = v_ref[0, :n_ctx, :].astype(BF16)

    def attend(q, parts):
        ss = []
        for kp, _, bp in parts:
            s = _dot_nt(q, kp)
            ss.append(s if bp is None else s + bp)
        m = functools.reduce(jnp.maximum, [jnp.max(s, axis=-1, keepdims=True) for s in ss])
        ps = [jnp.exp(s - m) for s in ss]
        den = functools.reduce(jnp.add, [jnp.sum(p, axis=-1, keepdims=True) for p in ps])
        num = functools.reduce(jnp.add, [_dot(p.astype(BF16), vp) for p, (_, vp, _) in zip(ps, parts)])
        return num / den

    qc = (q_ref[0, :n_ctx, :] * scale).astype(BF16)
    o_ref[0, :n_ctx, :] = attend(qc, [(kc, vc, None)])

    for j in range(n_blk):
        kb = min(max(Q_ROWS * j - WIN_R // 2, 0), rows - K_ROWS)
        var = 0 if j == 0 else (2 if j == n_blk - 1 else 1)
        q0 = n_ctx + j * nq
        k0 = n_ctx + kb * GRID_W
        q = (q_ref[0, q0:q0 + nq, :] * scale).astype(BF16)
        kw = k_ref[0, k0:k0 + nk, :].astype(BF16)
        vw = v_ref[0, k0:k0 + nk, :].astype(BF16)
        o_ref[0, q0:q0 + nq, :] = attend(q, [(kw, vw, bias_ref[0, var]), (kc, vc, None)])


def _attention(p, bias, n_ctx):
    b, l, _ = p.shape
    rows = (l - n_ctx) // GRID_W
    cb = OFF_CQKV // DK

    def pspec(off):
        return pl.BlockSpec((1, l, DK), lambda h, i: (i, 0, cb + off + h))

    return pl.pallas_call(
        functools.partial(_attn_kernel, n_ctx, rows),
        grid=(H_C, b),
        in_specs=[pspec(0), pspec(H_C), pspec(2 * H_C),
                  pl.BlockSpec((1,) + bias.shape[1:], lambda h, i: (h, 0, 0, 0))],
        out_specs=pl.BlockSpec((1, l, DK), lambda h, i: (i, 0, h)),
        out_shape=jax.ShapeDtypeStruct((b, l, H_C * DK), F32),
        compiler_params=_cparams("parallel", "parallel"),
        name="natten",
    )(p, p, p, bias)


def _attention_bias(rpb, rows):
    n_blk = rows // Q_ROWS
    out = []
    for j in (0, 1, n_blk - 1):
        kb = min(max(Q_ROWS * j - WIN_R // 2, 0), rows - K_ROWS)
        qr = (Q_ROWS * j + np.arange(Q_ROWS))[:, None, None, None]
        qc = np.arange(GRID_W)[None, :, None, None]
        kr = (kb + np.arange(K_ROWS))[None, None, :, None]
        kc = np.arange(GRID_W)[None, None, None, :]
        r0 = np.clip(qr - WIN_R // 2, 0, rows - WIN_R)
        c0 = np.clip(qc - WIN_C // 2, 0, GRID_W - WIN_C)
        ok = (kr >= r0) & (kr < r0 + WIN_R) & (kc >= c0) & (kc < c0 + WIN_C)
        dr = np.clip(kr - qr + WIN_R - 1, 0, 2 * WIN_R - 2)
        dc = np.clip(kc - qc, -(WIN_C - 1), WIN_C - 1) + WIN_C - 1
        shape = (Q_ROWS, GRID_W, K_ROWS, GRID_W)
        dr, dc, ok = (np.broadcast_to(t, shape).reshape(Q_ROWS * GRID_W, K_ROWS * GRID_W) for t in (dr, dc, ok))
        out.append(jnp.where(ok[None], rpb[:, dr, dc].astype(F32), NEG))
    return jnp.stack(out, axis=1)


def _merge_kernel(ctx_tiles, tile_off, oa_ref, hb_ref, oc_ref, ga_ref, gb_ref, gc_ref, m0_ref, m1_ref, m2_ref,
                  on_ref, wb_ref, wo_ref, x_ref, gp_ref, gl_ref, gcx_ref, o_ref):
    oa = oa_ref[0]
    parts = []
    for h in range(H_A):
        xh = oa[:, h * DK:(h + 1) * DK]
        parts.append(xh * lax.rsqrt(jnp.mean(xh * xh, axis=-1, keepdims=True) + EPS) * on_ref[...])
    ya = jnp.concatenate(parts, axis=-1) * _silu(ga_ref[0])
    yb = hb_ref[0] * _silu(gb_ref[0])
    yc = oc_ref[0] * _silu(gc_ref[0])
    merged = (jax.nn.sigmoid(m0_ref[0]) * _dot(ya.astype(BF16), wb_ref[0])
              + jax.nn.sigmoid(m1_ref[0]) * _dot(yb.astype(BF16), wb_ref[1])
              + jax.nn.sigmoid(m2_ref[0]) * _dot(yc.astype(BF16), wb_ref[2]))
    y = _dot(merged.astype(BF16), wo_ref[...])
    yn = y * lax.rsqrt(jnp.mean(y * y, axis=-1, keepdims=True) + EPS) * gp_ref[...]
    is_ctx = (pl.program_id(1) + tile_off) < ctx_tiles
    gate = jnp.where(is_ctx, gcx_ref[...], gl_ref[0])
    o_ref[0] = x_ref[0] + gate * yn


def _merge(oa, hb, oc, p, onorm, w_branch, w_out, xs, g_post, gate_lat, gate_ctx, n_ctx, lat_only):
    b, l, d = xs.shape
    tm = 256
    ctx_tiles = n_ctx // tm
    tile_off = ctx_tiles if lat_only else 0
    n_tiles = l // tm - tile_off

    def tok(width, col=0):
        return pl.BlockSpec((1, tm, width), lambda i, t: (i, t + tile_off, col))

    def const(shape):
        return pl.BlockSpec(shape, lambda i, t: (0,) * len(shape))

    return pl.pallas_call(
        functools.partial(_merge_kernel, ctx_tiles, tile_off),
        grid=(b, n_tiles),
        in_specs=[tok(W_BR), tok(W_BR), tok(W_BR),
                  tok(W_BR, OFF_AGATE // W_BR), tok(W_BR, OFF_BGATE // W_BR), tok(W_BR, OFF_CGATE // W_BR),
                  tok(d, OFF_MG // d), tok(d, OFF_MG // d + 1), tok(d, OFF_MG // d + 2),
                  const((1, DK)), const((3, W_BR, d)), const((d, d)),
                  tok(d), const((1, d)),
                  pl.BlockSpec((1, 1, d), lambda i, t: (i, 0, 0)), const((1, d))],
        out_specs=pl.BlockSpec((1, tm, d), lambda i, t: (i, t, 0)),
        out_shape=jax.ShapeDtypeStruct((b, n_tiles * tm, d), F32),
        compiler_params=_cparams("parallel", "parallel"),
        name="merge",
    )(oa, hb, oc, p, p, p, p, p, p, onorm, w_branch, w_out, xs, g_post, gate_lat, gate_ctx)


def _rope_tables(n_ctx, s_len):
    quarter = DK // 4
    pos = jnp.arange(s_len)
    inv_freq = ROPE_BASE ** (-jnp.arange(quarter, dtype=F32) / quarter)
    ang_r = (pos // GRID_W).astype(F32)[:, None] * inv_freq
    ang_c = (pos % GRID_W).astype(F32)[:, None] * inv_freq
    cos = jnp.concatenate([jnp.cos(ang_r)] * 2 + [jnp.cos(ang_c)] * 2, axis=-1)
    sin = jnp.concatenate([-jnp.sin(ang_r), jnp.sin(ang_r), -jnp.sin(ang_c), jnp.sin(ang_c)], axis=-1)
    cos = jnp.concatenate([jnp.ones((n_ctx, DK), F32), cos], axis=0)
    sin = jnp.concatenate([jnp.zeros((n_ctx, DK), F32), sin], axis=0)
    return cos, sin


def _block_diag(w):
    nb, bw, _ = w.shape
    return jnp.einsum("nde,nm->ndme", w, jnp.eye(nb, dtype=w.dtype)).reshape(nb * bw, nb * bw)


def kernel(x, c, ctx, c_ctx, w_mod, b_mod, g_pre, g_post, w_in, conv_a, a_log, dt_bias, onorm_a, conv_b,
           conv_b_bias, lru_wa, lru_ba, lru_wx, lru_bx, lru_lam, rpb, w_branch, w_out):
    bsz, s_len, d = x.shape
    n_ctx = ctx.shape[1]
    depth = w_in.shape[0]
    l = n_ctx + s_len
    n_chunks = l // CHUNK
    rows = s_len // GRID_W
    assert d == 1024 and n_ctx % 256 == 0 and s_len % (Q_ROWS * GRID_W) == 0 and rows >= K_ROWS

    pad = (-(bsz + 1)) % SUBLANES
    cvec = jnp.concatenate([c, c_ctx[None], jnp.zeros((pad, d), F32)], axis=0)
    mod = _modulation(cvec, w_mod, b_mod)
    cos_t, sin_t = _rope_tables(n_ctx, s_len)

    sm0 = 4 * W_BR
    w_main = jnp.concatenate([w_in[:, :, :sm0], w_in[:, :, sm0 + 2 * N_CHAIN:]], axis=-1).astype(BF16)
    w_small = jnp.pad(w_in[:, :, sm0:sm0 + 2 * N_CHAIN], ((0, 0), (0, 0), (0, N_SMALL - 2 * N_CHAIN))).astype(BF16)
    lane_pad = ((0, 0), (N_CHAIN, N_SMALL - 2 * N_CHAIN))
    alog_l = jnp.pad(a_log.reshape(depth, N_CHAIN), lane_pad)
    dt_l = jnp.pad(dt_bias.reshape(depth, N_CHAIN), lane_pad)
    w_br = w_branch.astype(BF16)
    w_o = w_out.astype(BF16)

    xs = jnp.concatenate([ctx, x], axis=1)
    for li in range(depth):
        last = li == depth - 1
        mod_lat = mod[li, :bsz, None, :]
        mod_ctx = mod[li, bsz:bsz + 1]
        p, psmall = _in_proj(xs, mod_lat, mod_ctx, g_pre[li][None], w_main[li], w_small[li], n_ctx)

        qn, kn, vn = _gdn_prep(p, conv_a[li], cos_t, sin_t, n_ctx)
        gsm = _gdn_gates(psmall, alog_l[li][None], dt_l[li][None])
        grow = gsm[:, :, N_CHAIN:2 * N_CHAIN].reshape(bsz, n_chunks, CHUNK, N_CHAIN).transpose(0, 3, 1, 2)
        oa = _gdn(qn, kn, vn, gsm, grow, n_ctx)

        w_bd = jnp.stack([jnp.concatenate([_block_diag(lru_wa[li, dd]), _block_diag(lru_wx[li, dd])], axis=-1)
                          for dd in range(2)]).astype(BF16)
        b_bd = jnp.concatenate([lru_ba[li], lru_bx[li]], axis=-1)[:, None, :]
        hb = _lru(p, conv_b[li], conv_b_bias[li][None], w_bd, b_bd, lru_lam[li][:, None, :], n_ctx)

        oc = _attention(p, _attention_bias(rpb[li], rows), n_ctx)

        xs = _merge(oa, hb, oc, p, onorm_a[li][None], w_br[li], w_o[li], xs, g_post[li][None],
                    mod[li, :bsz, None, 2 * d:], mod[li, bsz:bsz + 1, 2 * d:], n_ctx, last)
    return xs
```

```python
import functools
import math

import numpy as np
import jax
import jax.numpy as jnp
from jax import lax
from jax.experimental import pallas as pl
from jax.experimental.pallas import tpu as pltpu

F32 = jnp.float32
BF16 = jnp.bfloat16

EPS = 1e-6
GRID_W = 64
CONV_K = 4
H_A = 4
DK = 128
CHUNK = 64
N_CHAIN = 2 * H_A
W_BR = 512
NB_B = 8
RGLRU_C = 8.0
H_C = 4
WIN_R = 8
WIN_C = 16
ROPE_BASE = 10000.0
Q_ROWS = 4
K_ROWS = 12
NEG = -1e30
LRU_BLK = 64
SUBLANES = 8
VMEM_LIMIT = 56 * 1024 * 1024

OFF_AQKV, OFF_AGATE, OFF_BX, OFF_BGATE, OFF_CQKV, OFF_CGATE, OFF_MG = 0, 1536, 2048, 2560, 3072, 4608, 5120
N_MAIN = 8192
N_SMALL = 128


def _cparams(*sem):
    return pltpu.CompilerParams(dimension_semantics=sem, vmem_limit_bytes=VMEM_LIMIT)


def _dot(a, b):
    return jnp.dot(a, b, preferred_element_type=F32)


def _dot_nt(a, b):
    return lax.dot_general(a, b, (((1,), (1,)), ((), ())), preferred_element_type=F32)


def _silu(x):
    return x * jax.nn.sigmoid(x)


def _softplus(x):
    return jnp.maximum(x, 0.0) + jnp.log1p(jnp.exp(-jnp.abs(x)))


def _mod_kernel(c_ref, w_ref, b_ref, o_ref):
    o_ref[0] = _dot(_silu(c_ref[...]).astype(BF16), w_ref[0].astype(BF16)) + b_ref[0]


def _modulation(cvec, w_mod, b_mod):
    depth, d, n = w_mod.shape
    rows = cvec.shape[0]
    tn = 768
    return pl.pallas_call(
        _mod_kernel,
        grid=(depth, n // tn),
        in_specs=[pl.BlockSpec((rows, d), lambda l, j: (0, 0)),
                  pl.BlockSpec((1, d, tn), lambda l, j: (l, 0, j)),
                  pl.BlockSpec((1, 1, tn), lambda l, j: (l, 0, j))],
        out_specs=pl.BlockSpec((1, rows, tn), lambda l, j: (l, 0, j)),
        out_shape=jax.ShapeDtypeStruct((depth, rows, n), F32),
        compiler_params=_cparams("parallel", "parallel"),
        name="modulation",
    )(cvec, w_mod, b_mod.reshape(depth, 1, n))


def _inproj_kernel(n_ctx, x_ref, ml_ref, mc_ref, g_ref, w_ref, ws_ref, o_ref, os_ref, h_ref):
    d = x_ref.shape[-1]

    @pl.when(pl.program_id(1) == 0)
    def _():
        g = g_ref[...]

        def normed(x, mod):
            y = x * lax.rsqrt(jnp.mean(x * x, axis=-1, keepdims=True) + EPS) * g
            return (y * (1.0 + mod[:, d:2 * d]) + mod[:, :d]).astype(BF16)

        h_ref[:n_ctx] = normed(x_ref[0, :n_ctx], mc_ref[...])
        h_ref[n_ctx:] = normed(x_ref[0, n_ctx:], ml_ref[0])
        os_ref[0] = _dot(h_ref[...], ws_ref[...])

    o_ref[0] = _dot(h_ref[...], w_ref[...])


def _in_proj(xs, mod_lat, mod_ctx, g_pre, w_main, w_small, n_ctx):
    b, l, d = xs.shape
    tn = 512
    return pl.pallas_call(
        functools.partial(_inproj_kernel, n_ctx),
        grid=(b, N_MAIN // tn),
        in_specs=[pl.BlockSpec((1, l, d), lambda i, j: (i, 0, 0)),
                  pl.BlockSpec((1, 1, 3 * d), lambda i, j: (i, 0, 0)),
                  pl.BlockSpec((1, 3 * d), lambda i, j: (0, 0)),
                  pl.BlockSpec((1, d), lambda i, j: (0, 0)),
                  pl.BlockSpec((d, tn), lambda i, j: (0, j)),
                  pl.BlockSpec((d, N_SMALL), lambda i, j: (0, 0))],
        out_specs=[pl.BlockSpec((1, l, tn), lambda i, j: (i, 0, j)),
                   pl.BlockSpec((1, l, N_SMALL), lambda i, j: (i, 0, 0))],
        out_shape=[jax.ShapeDtypeStruct((b, l, N_MAIN), F32),
                   jax.ShapeDtypeStruct((b, l, N_SMALL), F32)],
        scratch_shapes=[pltpu.VMEM((l, d), BF16)],
        compiler_params=_cparams("parallel", "arbitrary"),
        name="in_proj",
    )(xs, mod_lat, mod_ctx, g_pre, w_main, w_small)


def _conv_centred(x, w, n_ctx):
    l = x.shape[0]
    t = lax.broadcasted_iota(jnp.int32, (l, 1), 0)
    tl = jnp.where(t < n_ctx, t, t - n_ctx)
    rem = jnp.where(t < n_ctx, n_ctx - 1 - t, l - 1 - t)
    acc = x * w[CONV_K // 2:CONV_K // 2 + 1]
    for k in range(CONV_K):
        off = k - CONV_K // 2
        if off == 0:
            continue
        shifted = pltpu.roll(x, (-off) % l, 0)
        ok = (tl >= -off) if off < 0 else (rem >= off)
        acc = acc + jnp.where(ok, shifted, 0.0) * w[k:k + 1]
    return acc


def _gdn_prep_kernel(n_ctx, pq_ref, pk_ref, pv_ref, wq_ref, wk_ref, wv_ref, cos_ref, sin_ref,
                     q_ref, k_ref, v_ref):
    lane = lax.broadcasted_iota(jnp.int32, (1, DK), 1)
    first = (lane % (DK // 2)) < (DK // 4)

    def qk_path(p_ref, w_ref, scale):
        x = _silu(_conv_centred(p_ref[0], w_ref[...], n_ctx))
        x = x * lax.rsqrt(jnp.sum(x * x, axis=-1, keepdims=True) + EPS)
        partner = jnp.where(first, pltpu.roll(x, DK - DK // 4, 1), pltpu.roll(x, DK // 4, 1))
        x = x * cos_ref[...] + partner * sin_ref[...]
        if scale != 1.0:
            x = x * scale
        return x.astype(BF16)

    q_ref[0] = qk_path(pq_ref, wq_ref, DK ** -0.5)
    k_ref[0] = qk_path(pk_ref, wk_ref, 1.0)
    v_ref[0] = _silu(_conv_centred(pv_ref[0], wv_ref[...], n_ctx)).astype(BF16)


def _gdn_prep(p, conv_a, cos_t, sin_t, n_ctx):
    b, l, _ = p.shape
    cb = OFF_AQKV // DK

    def pspec(off):
        return pl.BlockSpec((1, l, DK), lambda i, h: (i, 0, cb + off + h))

    def wspec(off):
        return pl.BlockSpec((CONV_K, DK), lambda i, h: (0, off + h))

    tab = pl.BlockSpec((l, DK), lambda i, h: (0, 0))
    out = pl.BlockSpec((1, l, DK), lambda i, h: (i, 0, h))
    shp = jax.ShapeDtypeStruct((b, l, H_A * DK), BF16)
    return pl.pallas_call(
        functools.partial(_gdn_prep_kernel, n_ctx),
        grid=(b, H_A),
        in_specs=[pspec(0), pspec(H_A), pspec(2 * H_A), wspec(0), wspec(H_A), wspec(2 * H_A), tab, tab],
        out_specs=[out, out, out],
        out_shape=[shp, shp, shp],
        compiler_params=_cparams("parallel", "parallel"),
        name="gdn_prep",
    )(p, p, p, conv_a, conv_a, conv_a, cos_t, sin_t)


def _gdn_gate_kernel(ps_ref, alog_ref, dt_ref, o_ref):
    x = ps_ref[0]
    l = x.shape[0]
    lane = lax.broadcasted_iota(jnp.int32, (1, N_SMALL), 1)
    g = -jnp.exp(alog_ref[...]) * _softplus(x + dt_ref[...])
    pos = lax.broadcasted_iota(jnp.int32, (l, 1), 0) % CHUNK
    fwd, bwd = g, g
    s = 1
    while s < CHUNK:
        fwd = fwd + jnp.where(pos >= s, pltpu.roll(fwd, s, 0), 0.0)
        bwd = bwd + jnp.where(pos < CHUNK - s, pltpu.roll(bwd, l - s, 0), 0.0)
        s *= 2
    gc = jnp.where(lane < 2 * H_A + H_A, fwd, bwd)
    o_ref[0] = jnp.where(lane < 2 * H_A, jax.nn.sigmoid(x), gc)


def _gdn_gates(psmall, alog_l, dt_l):
    b, l, n = psmall.shape
    vec = pl.BlockSpec((1, n), lambda i: (0, 0))
    return pl.pallas_call(
        _gdn_gate_kernel,
        grid=(b,),
        in_specs=[pl.BlockSpec((1, l, n), lambda i: (i, 0, 0)), vec, vec],
        out_specs=pl.BlockSpec((1, l, n), lambda i: (i, 0, 0)),
        out_shape=jax.ShapeDtypeStruct((b, l, n), F32),
        compiler_params=_cparams("parallel"),
        name="gdn_gates",
    )(psmall, alog_l, dt_l)


def _split_bf16(x):
    hi = x.astype(BF16)
    return hi, (x - hi.astype(F32)).astype(BF16)


def _dot3(a, b):
    ah, al = _split_bf16(a)
    bh, bl = _split_bf16(b)
    return _dot(ah, bh) + (_dot(ah, bl) + _dot(al, bh))


def _unit_tri_inverse(a):
    n = a.shape[0]
    ii = lax.broadcasted_iota(jnp.int32, (n, n), 0)
    jj = lax.broadcasted_iota(jnp.int32, (n, n), 1)
    p = -a
    t = jnp.where(ii == jj, 1.0, 0.0) + p
    s = 2
    while s < n:
        p = _dot3(p, p)
        t = t + _dot3(t, p)
        s *= 2
    return t


def _gdn_kernel(n_ctx, q_ref, k_ref, v_ref, gs_ref, gr_ref, o_ref, wv_ref, wk_ref, qk_ref, s_ref):
    l = q_ref.shape[1]
    n_chunks = l // CHUNK
    nc_ctx = n_ctx // CHUNK
    ii = lax.broadcasted_iota(jnp.int32, (CHUNK, CHUNK), 0)
    jj = lax.broadcasted_iota(jnp.int32, (CHUNK, CHUNK), 1)

    def chunk_inputs(c, d, h):
        rows = pl.ds(pl.multiple_of(c * CHUNK, CHUNK), CHUNK)
        ch = d * H_A + h
        gs = gs_ref[0, rows, :]
        beta = gs[:, ch:ch + 1]
        gcol = gs[:, N_CHAIN + ch:N_CHAIN + ch + 1]
        grow = gr_ref[0, ch, pl.ds(c, 1), :]
        return rows, beta, gcol, grow

    def prepare(c, carry):
        rows = pl.ds(pl.multiple_of(c * CHUNK, CHUNK), CHUNK)
        for h in range(H_A):
            cols = slice(h * DK, (h + 1) * DK)
            q, k, v = q_ref[0, rows, cols], k_ref[0, rows, cols], v_ref[0, rows, cols]
            kk = _dot_nt(k, k)
            qk = _dot_nt(q, k)
            kf, vf = k.astype(F32), v.astype(F32)
            for d in range(2):
                _, beta, gcol, grow = chunk_inputs(c, d, h)
                incl = (ii >= jj) if d == 0 else (ii <= jj)
                strict = (ii > jj) if d == 0 else (ii < jj)
                decay = jnp.where(incl, jnp.exp(jnp.where(incl, gcol - grow, 0.0)), 0.0)
                a = jnp.where(strict, kk * beta * decay, 0.0)
                t = _unit_tri_inverse(a)
                rhs = jnp.concatenate([vf * beta, kf * (beta * jnp.exp(gcol))], axis=-1)
                sol = _dot3(t, rhs)
                ch = d * H_A + h
                wv_ref[ch, rows, :] = sol[:, :DK].astype(BF16)
                wk_ref[ch, rows, :] = sol[:, DK:].astype(BF16)
                qk_ref[ch, rows, :] = (qk * decay).astype(BF16)
        return carry

    lax.fori_loop(0, n_chunks, prepare, 0)

    s_ref[...] = jnp.zeros_like(s_ref)
    o_ref[...] = jnp.zeros_like(o_ref)

    def recur(i, carry):
        c_bwd = jnp.where(i < nc_ctx, nc_ctx - 1 - i, n_chunks + nc_ctx - 1 - i)
        for d in range(2):
            c = i if d == 0 else c_bwd
            for h in range(H_A):
                cols = slice(h * DK, (h + 1) * DK)
                rows, _, gcol, grow = chunk_inputs(c, d, h)
                ch = d * H_A + h
                g_last = grow[:, CHUNK - 1:CHUNK] if d == 0 else grow[:, 0:1]
                s = s_ref[ch]
                s_b = s.astype(BF16)
                u = wv_ref[ch, rows, :].astype(F32) - _dot(wk_ref[ch, rows, :], s_b)
                u_b = u.astype(BF16)
                q_dec = (q_ref[0, rows, cols].astype(F32) * jnp.exp(gcol)).astype(BF16)
                o = _dot(q_dec, s_b) + _dot(qk_ref[ch, rows, :], u_b)
                k_tail = k_ref[0, rows, cols].astype(F32) * jnp.exp(g_last - gcol)
                s_ref[ch] = s * jnp.exp(g_last) + _dot(k_tail.T.astype(BF16), u_b)
                o_ref[0, rows, cols] += o
        return carry

    lax.fori_loop(0, n_chunks, recur, 0)


def _gdn(qn, kn, vn, gsm, grow, n_ctx):
    b, l, w = qn.shape
    n_chunks = l // CHUNK
    seq = pl.BlockSpec((1, l, w), lambda i: (i, 0, 0))
    return pl.pallas_call(
        functools.partial(_gdn_kernel, n_ctx),
        grid=(b,),
        in_specs=[seq, seq, seq,
                  pl.BlockSpec((1, l, N_SMALL), lambda i: (i, 0, 0)),
                  pl.BlockSpec((1, N_CHAIN, n_chunks, CHUNK), lambda i: (i, 0, 0, 0))],
        out_specs=pl.BlockSpec((1, l, w), lambda i: (i, 0, 0)),
        out_shape=jax.ShapeDtypeStruct((b, l, w), F32),
        scratch_shapes=[pltpu.VMEM((N_CHAIN, l, DK), BF16),
                        pltpu.VMEM((N_CHAIN, l, DK), BF16),
                        pltpu.VMEM((N_CHAIN, l, CHUNK), BF16),
                        pltpu.VMEM((N_CHAIN, DK, DK), F32)],
        compiler_params=_cparams("parallel"),
        name="gdn",
    )(qn, kn, vn, gsm, grow)


def _lru_kernel(n_ctx, x_ref, cw_ref, cb_ref, w_ref, b_ref, lam_ref, o_ref, xc_ref, h_ref):
    l = x_ref.shape[1]
    n_blk = l // LRU_BLK
    nb_ctx = n_ctx // LRU_BLK
    wd = x_ref.shape[2]
    xc_ref[...] = _conv_centred(x_ref[0], cw_ref[...], n_ctx) + cb_ref[...]
    h_ref[...] = jnp.zeros_like(h_ref)
    o_ref[...] = jnp.zeros_like(o_ref)
    sub = lax.broadcasted_iota(jnp.int32, (LRU_BLK, 1), 0) % SUBLANES

    def block(i, carry):
        c_bwd = jnp.where(i < nb_ctx, nb_ctx - 1 - i, n_blk + nb_ctx - 1 - i)
        for d in range(2):
            c = i if d == 0 else c_bwd
            rows = pl.ds(pl.multiple_of(c * LRU_BLK, LRU_BLK), LRU_BLK)
            x = xc_ref[rows, :]
            z = _dot(x.astype(BF16), w_ref[d]) + b_ref[d]
            r = jax.nn.sigmoid(z[:, :wd])
            gate_i = jax.nn.sigmoid(z[:, wd:])
            log_a = (-RGLRU_C) * r * _softplus(-lam_ref[d])
            a = jnp.exp(log_a)
            th = jnp.tanh(log_a)
            bb = jnp.sqrt(-2.0 * th / (1.0 - th)) * (gate_i * x)
            s = 1
            while s < SUBLANES:
                if d == 0:
                    ok = sub >= s
                    a_s, b_s = pltpu.roll(a, s, 0), pltpu.roll(bb, s, 0)
                else:
                    ok = sub < SUBLANES - s
                    a_s, b_s = pltpu.roll(a, LRU_BLK - s, 0), pltpu.roll(bb, LRU_BLK - s, 0)
                bb = bb + a * jnp.where(ok, b_s, 0.0)
                a = a * jnp.where(ok, a_s, 1.0)
                s *= 2
            hc = h_ref[d]
            n_grp = LRU_BLK // SUBLANES
            order = range(n_grp) if d == 0 else range(n_grp - 1, -1, -1)
            outs = [None] * n_grp
            for gi in order:
                sl = slice(gi * SUBLANES, (gi + 1) * SUBLANES)
                hg = bb[sl] + a[sl] * hc
                outs[gi] = hg
                hc = hg[SUBLANES - 1:SUBLANES] if d == 0 else hg[0:1]
            h_ref[d] = hc
            o_ref[0, rows, :] += jnp.concatenate(outs, axis=0)
        return carry

    lax.fori_loop(0, n_blk, block, 0)


def _lru(p, conv_b, conv_b_bias, w_bd, b_bd, lam, n_ctx):
    b, l, _ = p.shape
    return pl.pallas_call(
        functools.partial(_lru_kernel, n_ctx),
        grid=(b,),
        in_specs=[pl.BlockSpec((1, l, W_BR), lambda i: (i, 0, OFF_BX // W_BR)),
                  pl.BlockSpec((CONV_K, W_BR), lambda i: (0, 0)),
                  pl.BlockSpec((1, W_BR), lambda i: (0, 0)),
                  pl.BlockSpec((2, W_BR, 2 * W_BR), lambda i: (0, 0, 0)),
                  pl.BlockSpec((2, 1, 2 * W_BR), lambda i: (0, 0, 0)),
                  pl.BlockSpec((2, 1, W_BR), lambda i: (0, 0, 0))],
        out_specs=pl.BlockSpec((1, l, W_BR), lambda i: (i, 0, 0)),
        out_shape=jax.ShapeDtypeStruct((b, l, W_BR), F32),
        scratch_shapes=[pltpu.VMEM((l, W_BR), F32), pltpu.VMEM((2, 1, W_BR), F32)],
        compiler_params=_cparams("parallel"),
        name="rglru",
    )(p, conv_b, conv_b_bias, w_bd, b_bd, lam)


def _attn_kernel(n_ctx, rows, q_ref, k_ref, v_ref, bias_ref, o_ref):
    scale = DK ** -0.5
    n_blk = rows // Q_ROWS
    nq = Q_ROWS * GRID_W
    nk = K_ROWS * GRID_W
    kc = k_ref[0, :n_ctx, :].astype(BF16)
    vc = v_ref[0, :n_ctx, :].astype(BF16)

    def attend(q, parts):
        ss = []
        for kp, _, bp in parts:
            s = _dot_nt(q, kp)
            ss.append(s if bp is None else s + bp)
        m = functools.reduce(jnp.maximum, [jnp.max(s, axis=-1, keepdims=True) for s in ss])
        ps = [jnp.exp(s - m) for s in ss]
        den = functools.reduce(jnp.add, [jnp.sum(p, axis=-1, keepdims=True) for p in ps])
        num = functools.reduce(jnp.add, [_dot(p.astype(BF16), vp) for p, (_, vp, _) in zip(ps, parts)])
        return num / den

    qc = (q_ref[0, :n_ctx, :] * scale).astype(BF16)
    o_ref[0, :n_ctx, :] = attend(qc, [(kc, vc, None)])

    for j in range(n_blk):
        kb = min(max(Q_ROWS * j - WIN_R // 2, 0), rows - K_ROWS)
        var = 0 if j == 0 else (2 if j == n_blk - 1 else 1)
        q0 = n_ctx + j * nq
        k0 = n_ctx + kb * GRID_W
        q = (q_ref[0, q0:q0 + nq, :] * scale).astype(BF16)
        kw = k_ref[0, k0:k0 + nk, :].astype(BF16)
        vw = v_ref[0, k0:k0 + nk, :].astype(BF16)
        o_ref[0, q0:q0 + nq, :] = attend(q, [(kw, vw, bias_ref[0, var]), (kc, vc, None)])


def _attention(p, bias, n_ctx):
    b, l, _ = p.shape
    rows = (l - n_ctx) // GRID_W
    cb = OFF_CQKV // DK

    def pspec(off):
        return pl.BlockSpec((1, l, DK), lambda h, i: (i, 0, cb + off + h))

    return pl.pallas_call(
        functools.partial(_attn_kernel, n_ctx, rows),
        grid=(H_C, b),
        in_specs=[pspec(0), pspec(H_C), pspec(2 * H_C),
                  pl.BlockSpec((1,) + bias.shape[1:], lambda h, i: (h, 0, 0, 0))],
        out_specs=pl.BlockSpec((1, l, DK), lambda h, i: (i, 0, h)),
        out_shape=jax.ShapeDtypeStruct((b, l, H_C * DK), F32),
        compiler_params=_cparams("parallel", "parallel"),
        name="natten",
    )(p, p, p, bias)


def _attention_bias(rpb, rows):
    n_blk = rows // Q_ROWS
    qc = np.arange(GRID_W)[:, None]
    kc = np.arange(GRID_W)[None, :]
    c0 = np.clip(qc - WIN_C // 2, 0, GRID_W - WIN_C)
    col_ok = (kc >= c0) & (kc < c0 + WIN_C)
    dc = np.clip(kc - qc, -(WIN_C - 1), WIN_C - 1) + WIN_C - 1
    onehot = (dc[None] == np.arange(2 * WIN_C - 1)[:, None, None]).astype(np.float32)
    cols = jnp.einsum("...rd,dqk->...rqk", rpb.astype(F32), onehot, precision=lax.Precision.HIGHEST)
    cols = jnp.where(col_ok, cols, NEG)
    masked = jnp.full(cols.shape[:-3] + (GRID_W, GRID_W), NEG, F32)
    variants = []
    for j in (0, 1, n_blk - 1):
        kb = min(max(Q_ROWS * j - WIN_R // 2, 0), rows - K_ROWS)
        q_blocks = []
        for a in range(Q_ROWS):
            qr = Q_ROWS * j + a
            r0 = min(max(qr - WIN_R // 2, 0), rows - WIN_R)
            k_blocks = []
            for b in range(K_ROWS):
                kr = kb + b
                k_blocks.append(cols[..., kr - qr + WIN_R - 1, :, :] if r0 <= kr < r0 + WIN_R else masked)
            q_blocks.append(jnp.concatenate(k_blocks, axis=-1))
        variants.append(jnp.concatenate(q_blocks, axis=-2))
    return jnp.stack(variants, axis=-3)


def _merge_kernel(ctx_tiles, tile_off, oa_ref, hb_ref, oc_ref, ga_ref, gb_ref, gc_ref, m0_ref, m1_ref, m2_ref,
                  on_ref, wb_ref, wo_ref, x_ref, gp_ref, gl_ref, gcx_ref, o_ref):
    oa = oa_ref[0]
    parts = []
    for h in range(H_A):
        xh = oa[:, h * DK:(h + 1) * DK]
        parts.append(xh * lax.rsqrt(jnp.mean(xh * xh, axis=-1, keepdims=True) + EPS) * on_ref[...])
    ya = jnp.concatenate(parts, axis=-1) * _silu(ga_ref[0])
    yb = hb_ref[0] * _silu(gb_ref[0])
    yc = oc_ref[0] * _silu(gc_ref[0])
    merged = (jax.nn.sigmoid(m0_ref[0]) * _dot(ya.astype(BF16), wb_ref[0])
              + jax.nn.sigmoid(m1_ref[0]) * _dot(yb.astype(BF16), wb_ref[1])
              + jax.nn.sigmoid(m2_ref[0]) * _dot(yc.astype(BF16), wb_ref[2]))
    y = _dot(merged.astype(BF16), wo_ref[...])
    yn = y * lax.rsqrt(jnp.mean(y * y, axis=-1, keepdims=True) + EPS) * gp_ref[...]
    is_ctx = (pl.program_id(1) + tile_off) < ctx_tiles
    gate = jnp.where(is_ctx, gcx_ref[...], gl_ref[0])
    o_ref[0] = x_ref[0] + gate * yn


def _merge(oa, hb, oc, p, onorm, w_branch, w_out, xs, g_post, gate_lat, gate_ctx, n_ctx, lat_only):
    b, l, d = xs.shape
    tm = 256
    ctx_tiles = n_ctx // tm
    tile_off = ctx_tiles if lat_only else 0
    n_tiles = l // tm - tile_off

    def tok(width, col=0):
        return pl.BlockSpec((1, tm, width), lambda i, t: (i, t + tile_off, col))

    def const(shape):
        return pl.BlockSpec(shape, lambda i, t: (0,) * len(shape))

    return pl.pallas_call(
        functools.partial(_merge_kernel, ctx_tiles, tile_off),
        grid=(b, n_tiles),
        in_specs=[tok(W_BR), tok(W_BR), tok(W_BR),
                  tok(W_BR, OFF_AGATE // W_BR), tok(W_BR, OFF_BGATE // W_BR), tok(W_BR, OFF_CGATE // W_BR),
                  tok(d, OFF_MG // d), tok(d, OFF_MG // d + 1), tok(d, OFF_MG // d + 2),
                  const((1, DK)), const((3, W_BR, d)), const((d, d)),
                  tok(d), const((1, d)),
                  pl.BlockSpec((1, 1, d), lambda i, t: (i, 0, 0)), const((1, d))],
        out_specs=pl.BlockSpec((1, tm, d), lambda i, t: (i, t, 0)),
        out_shape=jax.ShapeDtypeStruct((b, n_tiles * tm, d), F32),
        compiler_params=_cparams("parallel", "parallel"),
        name="merge",
    )(oa, hb, oc, p, p, p, p, p, p, onorm, w_branch, w_out, xs, g_post, gate_lat, gate_ctx)


def _rope_tables(n_ctx, s_len):
    quarter = DK // 4
    pos = jnp.arange(s_len)
    inv_freq = ROPE_BASE ** (-jnp.arange(quarter, dtype=F32) / quarter)
    ang_r = (pos // GRID_W).astype(F32)[:, None] * inv_freq
    ang_c = (pos % GRID_W).astype(F32)[:, None] * inv_freq
    cos = jnp.concatenate([jnp.cos(ang_r)] * 2 + [jnp.cos(ang_c)] * 2, axis=-1)
    sin = jnp.concatenate([-jnp.sin(ang_r), jnp.sin(ang_r), -jnp.sin(ang_c), jnp.sin(ang_c)], axis=-1)
    cos = jnp.concatenate([jnp.ones((n_ctx, DK), F32), cos], axis=0)
    sin = jnp.concatenate([jnp.zeros((n_ctx, DK), F32), sin], axis=0)
    return cos, sin


def _block_diag(w):
    nb, bw, _ = w.shape
    return jnp.einsum("nde,nm->ndme", w, jnp.eye(nb, dtype=w.dtype)).reshape(nb * bw, nb * bw)


def kernel(x, c, ctx, c_ctx, w_mod, b_mod, g_pre, g_post, w_in, conv_a, a_log, dt_bias, onorm_a, conv_b,
           conv_b_bias, lru_wa, lru_ba, lru_wx, lru_bx, lru_lam, rpb, w_branch, w_out):
    bsz, s_len, d = x.shape
    n_ctx = ctx.shape[1]
    depth = w_in.shape[0]
    l = n_ctx + s_len
    n_chunks = l // CHUNK
    rows = s_len // GRID_W
    assert d == 1024 and n_ctx % 256 == 0 and s_len % (Q_ROWS * GRID_W) == 0 and rows >= K_ROWS

    pad = (-(bsz + 1)) % SUBLANES
    cvec = jnp.concatenate([c, c_ctx[None], jnp.zeros((pad, d), F32)], axis=0)
    mod = _modulation(cvec, w_mod, b_mod)
    cos_t, sin_t = _rope_tables(n_ctx, s_len)
    attn_bias = _attention_bias(rpb, rows)

    sm0 = 4 * W_BR
    w_main = jnp.concatenate([w_in[:, :, :sm0], w_in[:, :, sm0 + 2 * N_CHAIN:]], axis=-1).astype(BF16)
    w_small = jnp.pad(w_in[:, :, sm0:sm0 + 2 * N_CHAIN], ((0, 0), (0, 0), (0, N_SMALL - 2 * N_CHAIN))).astype(BF16)
    lane_pad = ((0, 0), (N_CHAIN, N_SMALL - 2 * N_CHAIN))
    alog_l = jnp.pad(a_log.reshape(depth, N_CHAIN), lane_pad)
    dt_l = jnp.pad(dt_bias.reshape(depth, N_CHAIN), lane_pad)
    w_br = w_branch.astype(BF16)
    w_o = w_out.astype(BF16)

    xs = jnp.concatenate([ctx, x], axis=1)
    for li in range(depth):
        last = li == depth - 1
        mod_lat = mod[li, :bsz, None, :]
        mod_ctx = mod[li, bsz:bsz + 1]
        p, psmall = _in_proj(xs, mod_lat, mod_ctx, g_pre[li][None], w_main[li], w_small[li], n_ctx)

        qn, kn, vn = _gdn_prep(p, conv_a[li], cos_t, sin_t, n_ctx)
        gsm = _gdn_gates(psmall, alog_l[li][None], dt_l[li][None])
        grow = gsm[:, :, N_CHAIN:2 * N_CHAIN].reshape(bsz, n_chunks, CHUNK, N_CHAIN).transpose(0, 3, 1, 2)
        oa = _gdn(qn, kn, vn, gsm, grow, n_ctx)

        w_bd = jnp.stack([jnp.concatenate([_block_diag(lru_wa[li, dd]), _block_diag(lru_wx[li, dd])], axis=-1)
                          for dd in range(2)]).astype(BF16)
        b_bd = jnp.concatenate([lru_ba[li], lru_bx[li]], axis=-1)[:, None, :]
        hb = _lru(p, conv_b[li], conv_b_bias[li][None], w_bd, b_bd, lru_lam[li][:, None, :], n_ctx)

        oc = _attention(p, attn_bias[li], n_ctx)

        xs = _merge(oa, hb, oc, p, onorm_a[li][None], w_br[li], w_o[li], xs, g_post[li][None],
                    mod[li, :bsz, None, 2 * d:], mod[li, bsz:bsz + 1, 2 * d:], n_ctx, last)
    return xs
```

```python
import functools
import math

import numpy as np
import jax
import jax.numpy as jnp
from jax import lax
from jax.experimental import pallas as pl
from jax.experimental.pallas import tpu as pltpu

F32 = jnp.float32
BF16 = jnp.bfloat16

EPS = 1e-6
GRID_W = 64
CONV_K = 4
H_A = 4
DK = 128
CHUNK = 64
N_CHAIN = 2 * H_A
INV_BLK = 16
PREP_CHUNKS = 4
W_BR = 512
NB_B = 8
RGLRU_C = 8.0
H_C = 4
WIN_R = 8
WIN_C = 16
ROPE_BASE = 10000.0
Q_ROWS = 4
K_ROWS = 12
NEG = -1e30
LRU_BLK = 64
SUBLANES = 8
LANES = 128
VMEM_LIMIT = 56 * 1024 * 1024

OFF_AQKV, OFF_AGATE, OFF_BX, OFF_BGATE, OFF_CQKV, OFF_CGATE, OFF_MG = 0, 1536, 2048, 2560, 3072, 4608, 5120
N_MAIN = 8192
N_SMALL = 128


def _cparams(*sem):
    return pltpu.CompilerParams(dimension_semantics=sem, vmem_limit_bytes=VMEM_LIMIT)


def _dot(a, b):
    return jnp.dot(a, b, preferred_element_type=F32)


def _dot_nt(a, b):
    return lax.dot_general(a, b, (((1,), (1,)), ((), ())), preferred_element_type=F32)


def _silu(x):
    return x * jax.nn.sigmoid(x)


def _softplus(x):
    return jnp.maximum(x, 0.0) + jnp.log1p(jnp.exp(-jnp.abs(x)))


def _mod_kernel(c_ref, w_ref, b_ref, o_ref):
    o_ref[0] = _dot(_silu(c_ref[...]).astype(BF16), w_ref[0].astype(BF16)) + b_ref[0]


def _modulation(cvec, w_mod, b_mod):
    depth, d, n = w_mod.shape
    rows = cvec.shape[0]
    tn = 768
    return pl.pallas_call(
        _mod_kernel,
        grid=(depth, n // tn),
        in_specs=[pl.BlockSpec((rows, d), lambda l, j: (0, 0)),
                  pl.BlockSpec((1, d, tn), lambda l, j: (l, 0, j)),
                  pl.BlockSpec((1, 1, tn), lambda l, j: (l, 0, j))],
        out_specs=pl.BlockSpec((1, rows, tn), lambda l, j: (l, 0, j)),
        out_shape=jax.ShapeDtypeStruct((depth, rows, n), F32),
        compiler_params=_cparams("parallel", "parallel"),
        name="modulation",
    )(cvec, w_mod, b_mod.reshape(depth, 1, n))


def _inproj_kernel(n_ctx, x_ref, ml_ref, mc_ref, g_ref, w_ref, ws_ref, o_ref, os_ref, h_ref):
    d = x_ref.shape[-1]

    @pl.when(pl.program_id(1) == 0)
    def _():
        g = g_ref[...]

        def normed(x, mod):
            y = x * lax.rsqrt(jnp.mean(x * x, axis=-1, keepdims=True) + EPS) * g
            return (y * (1.0 + mod[:, d:2 * d]) + mod[:, :d]).astype(BF16)

        h_ref[:n_ctx] = normed(x_ref[0, :n_ctx], mc_ref[...])
        h_ref[n_ctx:] = normed(x_ref[0, n_ctx:], ml_ref[0])
        os_ref[0] = _dot(h_ref[...], ws_ref[...])

    o_ref[0] = _dot(h_ref[...], w_ref[...]).astype(o_ref.dtype)


def _in_proj(xs, mod_lat, mod_ctx, g_pre, w_main, w_small, n_ctx):
    b, l, d = xs.shape
    tn = 512
    return pl.pallas_call(
        functools.partial(_inproj_kernel, n_ctx),
        grid=(b, N_MAIN // tn),
        in_specs=[pl.BlockSpec((1, l, d), lambda i, j: (i, 0, 0)),
                  pl.BlockSpec((1, 1, 3 * d), lambda i, j: (i, 0, 0)),
                  pl.BlockSpec((1, 3 * d), lambda i, j: (0, 0)),
                  pl.BlockSpec((1, d), lambda i, j: (0, 0)),
                  pl.BlockSpec((d, tn), lambda i, j: (0, j)),
                  pl.BlockSpec((d, N_SMALL), lambda i, j: (0, 0))],
        out_specs=[pl.BlockSpec((1, l, tn), lambda i, j: (i, 0, j)),
                   pl.BlockSpec((1, l, N_SMALL), lambda i, j: (i, 0, 0))],
        out_shape=[jax.ShapeDtypeStruct((b, l, N_MAIN), BF16),
                   jax.ShapeDtypeStruct((b, l, N_SMALL), F32)],
        scratch_shapes=[pltpu.VMEM((l, d), BF16)],
        compiler_params=_cparams("parallel", "arbitrary"),
        name="in_proj",
    )(xs, mod_lat, mod_ctx, g_pre, w_main, w_small)


def _conv_centred(x, w, n_ctx):
    l = x.shape[0]
    t = lax.broadcasted_iota(jnp.int32, (l, 1), 0)
    tl = jnp.where(t < n_ctx, t, t - n_ctx)
    rem = jnp.where(t < n_ctx, n_ctx - 1 - t, l - 1 - t)
    acc = x * w[CONV_K // 2:CONV_K // 2 + 1]
    for k in range(CONV_K):
        off = k - CONV_K // 2
        if off == 0:
            continue
        shifted = pltpu.roll(x, (-off) % l, 0)
        ok = (tl >= -off) if off < 0 else (rem >= off)
        acc = acc + jnp.where(ok, shifted, 0.0) * w[k:k + 1]
    return acc


def _gdn_prep_kernel(n_ctx, pq_ref, pk_ref, pv_ref, wq_ref, wk_ref, wv_ref, cos_ref, sin_ref,
                     q_ref, k_ref, v_ref):
    lane = lax.broadcasted_iota(jnp.int32, (1, DK), 1)
    first = (lane % (DK // 2)) < (DK // 4)

    def qk_path(p_ref, w_ref, scale):
        x = _silu(_conv_centred(p_ref[0].astype(F32), w_ref[...], n_ctx))
        x = x * lax.rsqrt(jnp.sum(x * x, axis=-1, keepdims=True) + EPS)
        partner = jnp.where(first, pltpu.roll(x, DK - DK // 4, 1), pltpu.roll(x, DK // 4, 1))
        x = x * cos_ref[...] + partner * sin_ref[...]
        if scale != 1.0:
            x = x * scale
        return x.astype(BF16)

    q_ref[0] = qk_path(pq_ref, wq_ref, DK ** -0.5)
    k_ref[0] = qk_path(pk_ref, wk_ref, 1.0)
    v_ref[0] = _silu(_conv_centred(pv_ref[0].astype(F32), wv_ref[...], n_ctx)).astype(BF16)


def _gdn_prep(p, conv_a, cos_t, sin_t, n_ctx):
    b, l, _ = p.shape
    cb = OFF_AQKV // DK

    def pspec(off):
        return pl.BlockSpec((1, l, DK), lambda i, h: (i, 0, cb + off + h))

    def wspec(off):
        return pl.BlockSpec((CONV_K, DK), lambda i, h: (0, off + h))

    tab = pl.BlockSpec((l, DK), lambda i, h: (0, 0))
    out = pl.BlockSpec((1, l, DK), lambda i, h: (i, 0, h))
    shp = jax.ShapeDtypeStruct((b, l, H_A * DK), BF16)
    return pl.pallas_call(
        functools.partial(_gdn_prep_kernel, n_ctx),
        grid=(b, H_A),
        in_specs=[pspec(0), pspec(H_A), pspec(2 * H_A), wspec(0), wspec(H_A), wspec(2 * H_A), tab, tab],
        out_specs=[out, out, out],
        out_shape=[shp, shp, shp],
        compiler_params=_cparams("parallel", "parallel"),
        name="gdn_prep",
    )(p, p, p, conv_a, conv_a, conv_a, cos_t, sin_t)


def _gdn_gate_kernel(ps_ref, alog_ref, dt_ref, o_ref):
    x = ps_ref[0]
    l = x.shape[0]
    lane = lax.broadcasted_iota(jnp.int32, (1, N_SMALL), 1)
    g = -jnp.exp(alog_ref[...]) * _softplus(x + dt_ref[...])
    pos = lax.broadcasted_iota(jnp.int32, (l, 1), 0) % CHUNK
    fwd, bwd = g, g
    s = 1
    while s < CHUNK:
        fwd = fwd + jnp.where(pos >= s, pltpu.roll(fwd, s, 0), 0.0)
        bwd = bwd + jnp.where(pos < CHUNK - s, pltpu.roll(bwd, l - s, 0), 0.0)
        s *= 2
    gc = jnp.where(lane < 2 * H_A + H_A, fwd, bwd)
    o_ref[0] = jnp.where(lane < 2 * H_A, jax.nn.sigmoid(x), gc)


def _gdn_gates(psmall, alog_l, dt_l):
    b, l, n = psmall.shape
    vec = pl.BlockSpec((1, n), lambda i: (0, 0))
    return pl.pallas_call(
        _gdn_gate_kernel,
        grid=(b,),
        in_specs=[pl.BlockSpec((1, l, n), lambda i: (i, 0, 0)), vec, vec],
        out_specs=pl.BlockSpec((1, l, n), lambda i: (i, 0, 0)),
        out_shape=jax.ShapeDtypeStruct((b, l, n), F32),
        compiler_params=_cparams("parallel"),
        name="gdn_gates",
    )(psmall, alog_l, dt_l)


def _head_block_mask(shape, row_blk, col_blk):
    r = lax.broadcasted_iota(jnp.int32, shape, 0) // row_blk
    c = lax.broadcasted_iota(jnp.int32, shape, 1) // col_blk
    return r == c


def _block_diag_tile(x, reps, mask):
    return jnp.where(mask, jnp.concatenate([x] * reps, axis=0), jnp.zeros((), x.dtype))


def _dot_tn(a, b):
    return lax.dot_general(a, b, (((0,), (0,)), ((), ())), preferred_element_type=F32)


def _gdn_kernel(n_ctx, q_ref, k_ref, v_ref, gs_ref, gr_ref, br_ref, o_ref, wv_ref, wk_ref, qk_ref, s_ref):
    l = q_ref.shape[1]
    n_chunks = l // CHUNK
    nc_ctx = n_ctx // CHUNK
    wide = H_A * CHUNK
    n_levels = INV_BLK.bit_length() - 1
    ii = lax.broadcasted_iota(jnp.int32, (CHUNK, wide), 0)
    jw = lax.broadcasted_iota(jnp.int32, (CHUNK, wide), 1)
    jj, hw = jw % CHUNK, jw // CHUNK
    eye_w = jnp.where(ii == jj, 1.0, 0.0)
    m_kbd = _head_block_mask((wide, H_A * DK), CHUNK, DK)
    m_pbd = _head_block_mask((wide, wide), CHUNK, CHUNK)
    m_ubd = _head_block_mask((2 * CHUNK, 2 * DK), CHUNK, DK)
    m_sbd = _head_block_mask((2 * DK, 2 * DK), DK, DK)

    def chunk_rows(c):
        return pl.ds(pl.multiple_of(c * CHUNK, CHUNK), CHUNK)

    def heads_to_rows(x):
        return jnp.concatenate([x[:, h * DK:(h + 1) * DK] for h in range(H_A)], axis=0)

    def prepare(it, carry):
        streams = []
        for ci in range(PREP_CHUNKS):
            c = it * PREP_CHUNKS + ci
            rows = chunk_rows(c)
            q_all, k_all, v_all = q_ref[0, rows, :], k_ref[0, rows, :], v_ref[0, rows, :]
            gs = gs_ref[0, rows, :]
            kf = k_all.astype(F32)
            lhs = []
            for d in range(2):
                beta = jnp.concatenate([jnp.broadcast_to(gs[:, d * H_A + h:d * H_A + h + 1], (CHUNK, DK))
                                        for h in range(H_A)], axis=-1)
                lhs.append((kf * beta).astype(BF16))
            prod = _dot_nt(jnp.concatenate(lhs + [q_all], axis=0), _block_diag_tile(k_all, H_A, m_kbd))
            v_st, k_st = heads_to_rows(v_all), heads_to_rows(k_all)
            for d in range(2):
                lane0 = N_CHAIN + d * H_A
                gc_w = jnp.broadcast_to(gs[:, lane0 + H_A - 1:lane0 + H_A], (CHUNK, wide))
                for h in range(H_A - 2, -1, -1):
                    gc_w = jnp.where(hw == h, jnp.broadcast_to(gs[:, lane0 + h:lane0 + h + 1], (CHUNK, wide)), gc_w)
                g_row = gr_ref[0, d, pl.ds(c, 1), :]
                b_row = br_ref[0, d, pl.ds(c, 1), :]
                incl = (ii >= jj) if d == 0 else (ii <= jj)
                strict = (ii > jj) if d == 0 else (ii < jj)
                decay = jnp.where(incl, jnp.exp(jnp.where(incl, gc_w - g_row, 0.0)), 0.0)
                qk_ref[d, rows, :] = (prod[2 * CHUNK:] * decay).astype(BF16)
                a = jnp.where(strict, prod[d * CHUNK:(d + 1) * CHUNK] * decay, 0.0)
                p = jnp.where(ii // INV_BLK == jj // INV_BLK, -a, 0.0)
                streams.append(dict(d=d, rows=rows, a=a, p=p, t=eye_w + p, v_st=v_st, k_st=k_st,
                                    cv=b_row, ck=b_row * jnp.exp(g_row)))
        for st in streams:
            pb = st["p"].astype(BF16)
            st["p"] = _dot(pb, _block_diag_tile(pb, H_A, m_pbd))
        for _ in range(1, n_levels - 1):
            for st in streams:
                pb = st["p"].astype(BF16)
                r = _dot(jnp.concatenate([pb, st["t"].astype(BF16)], axis=0), _block_diag_tile(pb, H_A, m_pbd))
                st["p"], st["t"] = r[:CHUNK], st["t"] + r[CHUNK:]
        for st in streams:
            st["t"] = st["t"] + _dot(st["t"].astype(BF16), _block_diag_tile(st["p"].astype(BF16), H_A, m_pbd))
        blk = INV_BLK
        while blk < CHUNK:
            joins = (ii // (2 * blk) == jj // (2 * blk)) & (ii // blk != jj // blk)
            for st in streams:
                st["tb"] = st["t"].astype(BF16)
                st["w"] = _dot(jnp.where(joins, st["a"], 0.0).astype(BF16), _block_diag_tile(st["tb"], H_A, m_pbd))
            for st in streams:
                st["t"] = st["t"] - _dot(st["tb"], _block_diag_tile(st["w"].astype(BF16), H_A, m_pbd))
            blk *= 2
        for st in streams:
            tv = (st["t"] * st["cv"]).astype(BF16)
            tk = (st["t"] * st["ck"]).astype(BF16)
            st["wv"] = _dot(_block_diag_tile(tv, H_A, m_pbd), st["v_st"])
            st["wk"] = _dot(_block_diag_tile(tk, H_A, m_pbd), st["k_st"])
        for st in streams:
            for h in range(H_A):
                wv_ref[st["d"], st["rows"], h * DK:(h + 1) * DK] = st["wv"][h * CHUNK:(h + 1) * CHUNK].astype(BF16)
                wk_ref[st["d"], st["rows"], h * DK:(h + 1) * DK] = st["wk"][h * CHUNK:(h + 1) * CHUNK].astype(BF16)
        return carry

    lax.fori_loop(0, n_chunks // PREP_CHUNKS, prepare, 0)

    s_ref[...] = jnp.zeros_like(s_ref)
    o_ref[...] = jnp.zeros_like(o_ref)

    def recur(i, carry):
        c_bwd = jnp.where(i < nc_ctx, nc_ctx - 1 - i, n_chunks + nc_ctx - 1 - i)
        chains = []
        for d in range(2):
            c = i if d == 0 else c_bwd
            rows = chunk_rows(c)
            gs = gs_ref[0, rows, :]
            g_row = gr_ref[0, d, pl.ds(c, 1), :]
            last = CHUNK - 1 if d == 0 else 0
            for pr in range(H_A // 2):
                hs = (2 * pr, 2 * pr + 1)
                cols = slice(2 * pr * DK, (2 * pr + 2) * DK)
                gcol = [gs[:, N_CHAIN + d * H_A + h:N_CHAIN + d * H_A + h + 1] for h in hs]
                glast = [g_row[:, h * CHUNK + last:h * CHUNK + last + 1] for h in hs]
                e_q = jnp.concatenate([jnp.broadcast_to(jnp.exp(g), (CHUNK, DK)) for g in gcol], axis=-1)
                e_k = jnp.concatenate([jnp.broadcast_to(jnp.exp(gl - g), (CHUNK, DK))
                                       for g, gl in zip(gcol, glast)], axis=-1)
                e_s = jnp.concatenate([jnp.broadcast_to(jnp.exp(gl), (1, DK)) for gl in glast], axis=-1)
                q_dec = (q_ref[0, rows, cols].astype(F32) * e_q).astype(BF16)
                k_tail = (k_ref[0, rows, cols].astype(F32) * e_k).astype(BF16)
                chains.append(dict(d=d, pr=pr, rows=rows, cols=cols, e_s=e_s, q_dec=q_dec, k_tail=k_tail))
        for ch in chains:
            d, pr, rows, cols = ch["d"], ch["pr"], ch["rows"], ch["cols"]
            ch["s"] = s_ref[d, pr]
            ch["r"] = _dot(jnp.concatenate([wk_ref[d, rows, cols], ch["q_dec"]], axis=0), ch["s"].astype(BF16))
        for ch in chains:
            d, pr, rows, cols = ch["d"], ch["pr"], ch["rows"], ch["cols"]
            u_b = (wv_ref[d, rows, cols].astype(F32) - ch["r"][:CHUNK]).astype(BF16)
            ch["ku"] = _dot_tn(ch["k_tail"], u_b)
            ch["o"] = ch["r"][CHUNK:] + _dot(qk_ref[d, rows, 2 * pr * CHUNK:(2 * pr + 2) * CHUNK],
                                             _block_diag_tile(u_b, 2, m_ubd))
        for ch in chains:
            s_ref[ch["d"], ch["pr"]] = ch["s"] * ch["e_s"] + jnp.where(m_sbd, ch["ku"], 0.0)
            o_ref[0, ch["rows"], ch["cols"]] += ch["o"]
        return carry

    lax.fori_loop(0, n_chunks, recur, 0)


def _gdn(qn, kn, vn, gsm, g_rows, b_rows, n_ctx):
    b, l, w = qn.shape
    n_chunks = l // CHUNK
    wide = H_A * CHUNK
    seq = pl.BlockSpec((1, l, w), lambda i: (i, 0, 0))
    row = pl.BlockSpec((1, 2, n_chunks, wide), lambda i: (i, 0, 0, 0))
    return pl.pallas_call(
        functools.partial(_gdn_kernel, n_ctx),
        grid=(b,),
        in_specs=[seq, seq, seq, pl.BlockSpec((1, l, N_SMALL), lambda i: (i, 0, 0)), row, row],
        out_specs=pl.BlockSpec((1, l, w), lambda i: (i, 0, 0)),
        out_shape=jax.ShapeDtypeStruct((b, l, w), F32),
        scratch_shapes=[pltpu.VMEM((2, l, w), BF16),
                        pltpu.VMEM((2, l, w), BF16),
                        pltpu.VMEM((2, l, wide), BF16),
                        pltpu.VMEM((2, H_A // 2, 2 * DK, 2 * DK), F32)],
        compiler_params=_cparams("parallel"),
        name="gdn",
    )(qn, kn, vn, gsm, g_rows, b_rows)


def _lru_kernel(n_ctx, x_ref, cw_ref, cb_ref, w_ref, b_ref, lam_ref, o_ref, xc_ref, h_ref, hs_ref):
    l = x_ref.shape[1]
    n_blk = l // LRU_BLK
    nb_ctx = n_ctx // LRU_BLK
    wd = x_ref.shape[2]
    xc = _conv_centred(x_ref[0].astype(F32), cw_ref[...], n_ctx) + cb_ref[...]
    n_lg = wd // LANES
    for j in range(n_lg):
        xc_ref[j] = xc[:, j * LANES:(j + 1) * LANES]
    h_ref[...] = jnp.zeros_like(h_ref)
    o_ref[...] = jnp.zeros_like(o_ref)
    sub = lax.broadcasted_iota(jnp.int32, (SUBLANES, 1), 0)
    n_step = LRU_BLK // SUBLANES
    decay_rate = [(-RGLRU_C) * _softplus(-lam_ref[d]) for d in range(2)]

    def sigmoid(v):
        return 0.5 * jnp.tanh(0.5 * v) + 0.5

    def block(i, carry):
        c_bwd = jnp.where(i < nb_ctx, nb_ctx - 1 - i, n_blk + nb_ctx - 1 - i)
        for d in range(2):
            c = i if d == 0 else c_bwd
            base = pl.multiple_of(c * LRU_BLK, LRU_BLK)

            def step_rows(k):
                return pl.ds(base + k, SUBLANES, stride=n_step)

            x = jnp.concatenate([jnp.concatenate([xc_ref[j, step_rows(k), :] for j in range(n_lg)], axis=-1)
                                 for k in range(n_step)], axis=0)
            z = _dot(x.astype(BF16), w_ref[d]) + b_ref[d]
            log_a = sigmoid(z[:, :wd]) * decay_rate[d]
            a = jnp.exp(log_a)
            th = jnp.tanh(log_a)
            bb = jnp.sqrt(-2.0 * th / (1.0 - th)) * (sigmoid(z[:, wd:]) * x)
            order = list(range(n_step)) if d == 0 else list(range(n_step - 1, -1, -1))
            a_cum, b_cum = [None] * n_step, [None] * n_step
            for n, k in enumerate(order):
                sl = slice(k * SUBLANES, (k + 1) * SUBLANES)
                if n == 0:
                    a_cum[k], b_cum[k] = a[sl], bb[sl]
                else:
                    a_cum[k], b_cum[k] = a[sl] * a_cum[order[n - 1]], a[sl] * b_cum[order[n - 1]] + bb[sl]
            ga, gb = a_cum[order[-1]], b_cum[order[-1]]
            s = 1
            while s < SUBLANES:
                ok = (sub >= s) if d == 0 else (sub < SUBLANES - s)
                shift = s if d == 0 else SUBLANES - s
                gb = gb + ga * jnp.where(ok, pltpu.roll(gb, shift, 0), 0.0)
                ga = ga * jnp.where(ok, pltpu.roll(ga, shift, 0), 1.0)
                s *= 2
            h_in = jnp.broadcast_to(h_ref[d], (SUBLANES, wd))
            h_end = gb + ga * h_in
            if d == 0:
                h_grp = jnp.where(sub == 0, h_in, pltpu.roll(h_end, 1, 0))
                h_ref[d] = h_end[SUBLANES - 1:SUBLANES]
            else:
                h_grp = jnp.where(sub == SUBLANES - 1, h_in, pltpu.roll(h_end, SUBLANES - 1, 0))
                h_ref[d] = h_end[0:1]
            for k in range(n_step):
                h_k = b_cum[k] + a_cum[k] * h_grp
                for j in range(n_lg):
                    hs_ref[d, j, pl.ds(k, SUBLANES, stride=n_step), :] = h_k[:, j * LANES:(j + 1) * LANES]
            rows = pl.ds(base, LRU_BLK)
            o_ref[0, rows, :] += jnp.concatenate([hs_ref[d, j] for j in range(n_lg)], axis=-1)
        return carry

    lax.fori_loop(0, n_blk, block, 0)


def _lru(p, conv_b, conv_b_bias, w_bd, b_bd, lam, n_ctx):
    b, l, _ = p.shape
    return pl.pallas_call(
        functools.partial(_lru_kernel, n_ctx),
        grid=(b,),
        in_specs=[pl.BlockSpec((1, l, W_BR), lambda i: (i, 0, OFF_BX // W_BR)),
                  pl.BlockSpec((CONV_K, W_BR), lambda i: (0, 0)),
                  pl.BlockSpec((1, W_BR), lambda i: (0, 0)),
                  pl.BlockSpec((2, W_BR, 2 * W_BR), lambda i: (0, 0, 0)),
                  pl.BlockSpec((2, 1, 2 * W_BR), lambda i: (0, 0, 0)),
                  pl.BlockSpec((2, 1, W_BR), lambda i: (0, 0, 0))],
        out_specs=pl.BlockSpec((1, l, W_BR), lambda i: (i, 0, 0)),
        out_shape=jax.ShapeDtypeStruct((b, l, W_BR), F32),
        scratch_shapes=[pltpu.VMEM((W_BR // LANES, l, LANES), F32), pltpu.VMEM((2, 1, W_BR), F32),
                        pltpu.VMEM((2, W_BR // LANES, LRU_BLK, LANES), F32)],
        compiler_params=_cparams("parallel"),
        name="rglru",
    )(p, conv_b, conv_b_bias, w_bd, b_bd, lam)


def _attn_kernel(n_ctx, rows, q_ref, k_ref, v_ref, bias_ref, o_ref):
    scale = DK ** -0.5
    n_blk = rows // Q_ROWS
    nq = Q_ROWS * GRID_W
    nk = K_ROWS * GRID_W
    kc = k_ref[0, :n_ctx, :]
    vc = v_ref[0, :n_ctx, :]

    def attend(q, parts):
        ss = []
        for kp, _, bp in parts:
            s = _dot_nt(q, kp)
            ss.append(s if bp is None else s + bp)
        m = functools.reduce(jnp.maximum, [jnp.max(s, axis=-1, keepdims=True) for s in ss])
        ps = [jnp.exp(s - m) for s in ss]
        den = functools.reduce(jnp.add, [jnp.sum(p, axis=-1, keepdims=True) for p in ps])
        num = functools.reduce(jnp.add, [_dot(p.astype(BF16), vp) for p, (_, vp, _) in zip(ps, parts)])
        return num / den

    qc = (q_ref[0, :n_ctx, :].astype(F32) * scale).astype(BF16)
    o_ref[0, :n_ctx, :] = attend(qc, [(kc, vc, None)])

    for j in range(n_blk):
        kb = min(max(Q_ROWS * j - WIN_R // 2, 0), rows - K_ROWS)
        var = 0 if j == 0 else (2 if j == n_blk - 1 else 1)
        q0 = n_ctx + j * nq
        k0 = n_ctx + kb * GRID_W
        q = (q_ref[0, q0:q0 + nq, :].astype(F32) * scale).astype(BF16)
        kw = k_ref[0, k0:k0 + nk, :]
        vw = v_ref[0, k0:k0 + nk, :]
        o_ref[0, q0:q0 + nq, :] = attend(q, [(kw, vw, bias_ref[0, var]), (kc, vc, None)])


def _attention(p, bias, n_ctx):
    b, l, _ = p.shape
    rows = (l - n_ctx) // GRID_W
    cb = OFF_CQKV // DK

    def pspec(off):
        return pl.BlockSpec((1, l, DK), lambda h, i: (i, 0, cb + off + h))

    return pl.pallas_call(
        functools.partial(_attn_kernel, n_ctx, rows),
        grid=(H_C, b),
        in_specs=[pspec(0), pspec(H_C), pspec(2 * H_C),
                  pl.BlockSpec((1,) + bias.shape[1:], lambda h, i: (h, 0, 0, 0))],
        out_specs=pl.BlockSpec((1, l, DK), lambda h, i: (i, 0, h)),
        out_shape=jax.ShapeDtypeStruct((b, l, H_C * DK), F32),
        compiler_params=_cparams("parallel", "parallel"),
        name="natten",
    )(p, p, p, bias)


def _attention_bias(rpb, rows):
    n_blk = rows // Q_ROWS
    qc = np.arange(GRID_W)[:, None]
    kc = np.arange(GRID_W)[None, :]
    c0 = np.clip(qc - WIN_C // 2, 0, GRID_W - WIN_C)
    col_ok = (kc >= c0) & (kc < c0 + WIN_C)
    dc = np.clip(kc - qc, -(WIN_C - 1), WIN_C - 1) + WIN_C - 1
    onehot = (dc[None] == np.arange(2 * WIN_C - 1)[:, None, None]).astype(np.float32)
    cols = jnp.einsum("...rd,dqk->...rqk", rpb.astype(F32), onehot, precision=lax.Precision.HIGHEST)
    cols = jnp.where(col_ok, cols, NEG)
    masked = jnp.full(cols.shape[:-3] + (GRID_W, GRID_W), NEG, F32)
    variants = []
    for j in (0, 1, n_blk - 1):
        kb = min(max(Q_ROWS * j - WIN_R // 2, 0), rows - K_ROWS)
        q_blocks = []
        for a in range(Q_ROWS):
            qr = Q_ROWS * j + a
            r0 = min(max(qr - WIN_R // 2, 0), rows - WIN_R)
            k_blocks = []
            for b in range(K_ROWS):
                kr = kb + b
                k_blocks.append(cols[..., kr - qr + WIN_R - 1, :, :] if r0 <= kr < r0 + WIN_R else masked)
            q_blocks.append(jnp.concatenate(k_blocks, axis=-1))
        variants.append(jnp.concatenate(q_blocks, axis=-2))
    return jnp.stack(variants, axis=-3)


def _merge_kernel(ctx_tiles, tile_off, oa_ref, hb_ref, oc_ref, ga_ref, gb_ref, gc_ref, m0_ref, m1_ref, m2_ref,
                  on_ref, wb_ref, wo_ref, x_ref, gp_ref, gl_ref, gcx_ref, o_ref):
    oa = oa_ref[0]
    parts = []
    for h in range(H_A):
        xh = oa[:, h * DK:(h + 1) * DK]
        parts.append(xh * lax.rsqrt(jnp.mean(xh * xh, axis=-1, keepdims=True) + EPS) * on_ref[...])
    ya = jnp.concatenate(parts, axis=-1) * _silu(ga_ref[0].astype(F32))
    yb = hb_ref[0] * _silu(gb_ref[0].astype(F32))
    yc = oc_ref[0] * _silu(gc_ref[0].astype(F32))
    merged = (jax.nn.sigmoid(m0_ref[0].astype(F32)) * _dot(ya.astype(BF16), wb_ref[0])
              + jax.nn.sigmoid(m1_ref[0].astype(F32)) * _dot(yb.astype(BF16), wb_ref[1])
              + jax.nn.sigmoid(m2_ref[0].astype(F32)) * _dot(yc.astype(BF16), wb_ref[2]))
    y = _dot(merged.astype(BF16), wo_ref[...])
    yn = y * lax.rsqrt(jnp.mean(y * y, axis=-1, keepdims=True) + EPS) * gp_ref[...]
    is_ctx = (pl.program_id(1) + tile_off) < ctx_tiles
    gate = jnp.where(is_ctx, gcx_ref[...], gl_ref[0])
    o_ref[0] = x_ref[0] + gate * yn


def _merge(oa, hb, oc, p, onorm, w_branch, w_out, xs, g_post, gate_lat, gate_ctx, n_ctx, lat_only):
    b, l, d = xs.shape
    tm = 256
    ctx_tiles = n_ctx // tm
    tile_off = ctx_tiles if lat_only else 0
    n_tiles = l // tm - tile_off

    def tok(width, col=0):
        return pl.BlockSpec((1, tm, width), lambda i, t: (i, t + tile_off, col))

    def const(shape):
        return pl.BlockSpec(shape, lambda i, t: (0,) * len(shape))

    return pl.pallas_call(
        functools.partial(_merge_kernel, ctx_tiles, tile_off),
        grid=(b, n_tiles),
        in_specs=[tok(W_BR), tok(W_BR), tok(W_BR),
                  tok(W_BR, OFF_AGATE // W_BR), tok(W_BR, OFF_BGATE // W_BR), tok(W_BR, OFF_CGATE // W_BR),
                  tok(d, OFF_MG // d), tok(d, OFF_MG // d + 1), tok(d, OFF_MG // d + 2),
                  const((1, DK)), const((3, W_BR, d)), const((d, d)),
                  tok(d), const((1, d)),
                  pl.BlockSpec((1, 1, d), lambda i, t: (i, 0, 0)), const((1, d))],
        out_specs=pl.BlockSpec((1, tm, d), lambda i, t: (i, t, 0)),
        out_shape=jax.ShapeDtypeStruct((b, n_tiles * tm, d), F32),
        compiler_params=_cparams("parallel", "parallel"),
        name="merge",
    )(oa, hb, oc, p, p, p, p, p, p, onorm, w_branch, w_out, xs, g_post, gate_lat, gate_ctx)


def _rope_tables(n_ctx, s_len):
    quarter = DK // 4
    pos = jnp.arange(s_len)
    inv_freq = ROPE_BASE ** (-jnp.arange(quarter, dtype=F32) / quarter)
    ang_r = (pos // GRID_W).astype(F32)[:, None] * inv_freq
    ang_c = (pos % GRID_W).astype(F32)[:, None] * inv_freq
    cos = jnp.concatenate([jnp.cos(ang_r)] * 2 + [jnp.cos(ang_c)] * 2, axis=-1)
    sin = jnp.concatenate([-jnp.sin(ang_r), jnp.sin(ang_r), -jnp.sin(ang_c), jnp.sin(ang_c)], axis=-1)
    cos = jnp.concatenate([jnp.ones((n_ctx, DK), F32), cos], axis=0)
    sin = jnp.concatenate([jnp.zeros((n_ctx, DK), F32), sin], axis=0)
    return cos, sin


def _block_diag(w):
    nb, bw, _ = w.shape
    return jnp.einsum("nde,nm->ndme", w, jnp.eye(nb, dtype=w.dtype)).reshape(nb * bw, nb * bw)


def kernel(x, c, ctx, c_ctx, w_mod, b_mod, g_pre, g_post, w_in, conv_a, a_log, dt_bias, onorm_a, conv_b,
           conv_b_bias, lru_wa, lru_ba, lru_wx, lru_bx, lru_lam, rpb, w_branch, w_out):
    bsz, s_len, d = x.shape
    n_ctx = ctx.shape[1]
    depth = w_in.shape[0]
    l = n_ctx + s_len
    n_chunks = l // CHUNK
    rows = s_len // GRID_W
    assert d == 1024 and n_ctx % 256 == 0 and s_len % (Q_ROWS * GRID_W) == 0 and rows >= K_ROWS
    assert n_chunks % PREP_CHUNKS == 0

    pad = (-(bsz + 1)) % SUBLANES
    cvec = jnp.concatenate([c, c_ctx[None], jnp.zeros((pad, d), F32)], axis=0)
    mod = _modulation(cvec, w_mod, b_mod)
    cos_t, sin_t = _rope_tables(n_ctx, s_len)
    attn_bias = _attention_bias(rpb, rows)

    sm0 = 4 * W_BR
    w_main = jnp.concatenate([w_in[:, :, :sm0], w_in[:, :, sm0 + 2 * N_CHAIN:]], axis=-1).astype(BF16)
    w_small = jnp.pad(w_in[:, :, sm0:sm0 + 2 * N_CHAIN], ((0, 0), (0, 0), (0, N_SMALL - 2 * N_CHAIN))).astype(BF16)
    lane_pad = ((0, 0), (N_CHAIN, N_SMALL - 2 * N_CHAIN))
    alog_l = jnp.pad(a_log.reshape(depth, N_CHAIN), lane_pad)
    dt_l = jnp.pad(dt_bias.reshape(depth, N_CHAIN), lane_pad)
    w_br = w_branch.astype(BF16)
    w_o = w_out.astype(BF16)

    xs = jnp.concatenate([ctx, x], axis=1)
    for li in range(depth):
        last = li == depth - 1
        mod_lat = mod[li, :bsz, None, :]
        mod_ctx = mod[li, bsz:bsz + 1]
        p, psmall = _in_proj(xs, mod_lat, mod_ctx, g_pre[li][None], w_main[li], w_small[li], n_ctx)

        qn, kn, vn = _gdn_prep(p, conv_a[li], cos_t, sin_t, n_ctx)
        gsm = _gdn_gates(psmall, alog_l[li][None], dt_l[li][None])
        def rows_of(t):
            t = t.reshape(bsz, n_chunks, CHUNK, 2, H_A).transpose(0, 3, 1, 4, 2)
            return t.reshape(bsz, 2, n_chunks, H_A * CHUNK)

        oa = _gdn(qn, kn, vn, gsm, rows_of(gsm[:, :, N_CHAIN:2 * N_CHAIN]), rows_of(gsm[:, :, :N_CHAIN]), n_ctx)

        w_bd = jnp.stack([jnp.concatenate([_block_diag(lru_wa[li, dd]), _block_diag(lru_wx[li, dd])], axis=-1)
                          for dd in range(2)]).astype(BF16)
        b_bd = jnp.concatenate([lru_ba[li], lru_bx[li]], axis=-1)[:, None, :]
        hb = _lru(p, conv_b[li], conv_b_bias[li][None], w_bd, b_bd, lru_lam[li][:, None, :], n_ctx)

        oc = _attention(p, attn_bias[li], n_ctx)

        xs = _merge(oa, hb, oc, p, onorm_a[li][None], w_br[li], w_o[li], xs, g_post[li][None],
                    mod[li, :bsz, None, 2 * d:], mod[li, bsz:bsz + 1, 2 * d:], n_ctx, last)
    return xs
```

```python
import functools
import math

import numpy as np
import jax
import jax.numpy as jnp
from jax import lax
from jax.experimental import pallas as pl
from jax.experimental.pallas import tpu as pltpu

F32 = jnp.float32
BF16 = jnp.bfloat16

EPS = 1e-6
GRID_W = 64
CONV_K = 4
H_A = 4
DK = 128
CHUNK = 64
N_CHAIN = 2 * H_A
INV_BLK = 16
PREP_CHUNKS = 4
W_BR = 512
NB_B = 8
RGLRU_C = 8.0
H_C = 4
WIN_R = 8
WIN_C = 16
ROPE_BASE = 10000.0
Q_ROWS = 4
K_ROWS = 12
NEG = -1e30
LRU_BLK = 64
SUBLANES = 8
LANES = 128
VMEM_LIMIT = 56 * 1024 * 1024

OFF_AQKV, OFF_AGATE, OFF_BX, OFF_BGATE, OFF_CQKV, OFF_CGATE, OFF_MG = 0, 1536, 2048, 2560, 3072, 4608, 5120
N_MAIN = 8192
N_SMALL = 128


def _cparams(*sem):
    return pltpu.CompilerParams(dimension_semantics=sem, vmem_limit_bytes=VMEM_LIMIT)


def _dot(a, b):
    return jnp.dot(a, b, preferred_element_type=F32)


def _dot_nt(a, b):
    return lax.dot_general(a, b, (((1,), (1,)), ((), ())), preferred_element_type=F32)


def _silu(x):
    return x * jax.nn.sigmoid(x)


def _softplus(x):
    return jnp.maximum(x, 0.0) + jnp.log1p(jnp.exp(-jnp.abs(x)))


def _mod_kernel(c_ref, w_ref, b_ref, o_ref):
    o_ref[0] = _dot(_silu(c_ref[...]).astype(BF16), w_ref[0].astype(BF16)) + b_ref[0]


def _modulation(cvec, w_mod, b_mod):
    depth, d, n = w_mod.shape
    rows = cvec.shape[0]
    tn = 768
    return pl.pallas_call(
        _mod_kernel,
        grid=(depth, n // tn),
        in_specs=[pl.BlockSpec((rows, d), lambda l, j: (0, 0)),
                  pl.BlockSpec((1, d, tn), lambda l, j: (l, 0, j)),
                  pl.BlockSpec((1, 1, tn), lambda l, j: (l, 0, j))],
        out_specs=pl.BlockSpec((1, rows, tn), lambda l, j: (l, 0, j)),
        out_shape=jax.ShapeDtypeStruct((depth, rows, n), F32),
        compiler_params=_cparams("parallel", "parallel"),
        name="modulation",
    )(cvec, w_mod, b_mod.reshape(depth, 1, n))


def _inproj_kernel(n_ctx, x_ref, ml_ref, mc_ref, g_ref, w_ref, ws_ref, o_ref, os_ref, h_ref):
    d = x_ref.shape[-1]

    @pl.when(pl.program_id(1) == 0)
    def _():
        g = g_ref[...]

        def normed(x, mod):
            y = x * lax.rsqrt(jnp.mean(x * x, axis=-1, keepdims=True) + EPS) * g
            return (y * (1.0 + mod[:, d:2 * d]) + mod[:, :d]).astype(BF16)

        h_ref[:n_ctx] = normed(x_ref[0, :n_ctx], mc_ref[...])
        h_ref[n_ctx:] = normed(x_ref[0, n_ctx:], ml_ref[0])
        os_ref[0] = _dot(h_ref[...], ws_ref[...])

    o_ref[0] = _dot(h_ref[...], w_ref[...]).astype(o_ref.dtype)


def _in_proj(xs, mod_lat, mod_ctx, g_pre, w_main, w_small, n_ctx):
    b, l, d = xs.shape
    tn = 512
    return pl.pallas_call(
        functools.partial(_inproj_kernel, n_ctx),
        grid=(b, N_MAIN // tn),
        in_specs=[pl.BlockSpec((1, l, d), lambda i, j: (i, 0, 0)),
                  pl.BlockSpec((1, 1, 3 * d), lambda i, j: (i, 0, 0)),
                  pl.BlockSpec((1, 3 * d), lambda i, j: (0, 0)),
                  pl.BlockSpec((1, d), lambda i, j: (0, 0)),
                  pl.BlockSpec((d, tn), lambda i, j: (0, j)),
                  pl.BlockSpec((d, N_SMALL), lambda i, j: (0, 0))],
        out_specs=[pl.BlockSpec((1, l, tn), lambda i, j: (i, 0, j)),
                   pl.BlockSpec((1, l, N_SMALL), lambda i, j: (i, 0, 0))],
        out_shape=[jax.ShapeDtypeStruct((b, l, N_MAIN), BF16),
                   jax.ShapeDtypeStruct((b, l, N_SMALL), F32)],
        scratch_shapes=[pltpu.VMEM((l, d), BF16)],
        compiler_params=_cparams("parallel", "arbitrary"),
        name="in_proj",
    )(xs, mod_lat, mod_ctx, g_pre, w_main, w_small)


def _conv_centred(x, w, n_ctx):
    l = x.shape[0]
    t = lax.broadcasted_iota(jnp.int32, (l, 1), 0)
    tl = jnp.where(t < n_ctx, t, t - n_ctx)
    rem = jnp.where(t < n_ctx, n_ctx - 1 - t, l - 1 - t)
    acc = x * w[CONV_K // 2:CONV_K // 2 + 1]
    for k in range(CONV_K):
        off = k - CONV_K // 2
        if off == 0:
            continue
        shifted = pltpu.roll(x, (-off) % l, 0)
        ok = (tl >= -off) if off < 0 else (rem >= off)
        acc = acc + jnp.where(ok, shifted, 0.0) * w[k:k + 1]
    return acc


def _gdn_prep_kernel(n_ctx, pq_ref, pk_ref, pv_ref, wq_ref, wk_ref, wv_ref, cos_ref, sin_ref,
                     q_ref, k_ref, v_ref):
    lane = lax.broadcasted_iota(jnp.int32, (1, DK), 1)
    first = (lane % (DK // 2)) < (DK // 4)

    def qk_path(p_ref, w_ref, scale):
        x = _silu(_conv_centred(p_ref[0].astype(F32), w_ref[...], n_ctx))
        x = x * lax.rsqrt(jnp.sum(x * x, axis=-1, keepdims=True) + EPS)
        partner = jnp.where(first, pltpu.roll(x, DK - DK // 4, 1), pltpu.roll(x, DK // 4, 1))
        x = x * cos_ref[...] + partner * sin_ref[...]
        if scale != 1.0:
            x = x * scale
        return x.astype(BF16)

    q_ref[0] = qk_path(pq_ref, wq_ref, DK ** -0.5)
    k_ref[0] = qk_path(pk_ref, wk_ref, 1.0)
    v_ref[0] = _silu(_conv_centred(pv_ref[0].astype(F32), wv_ref[...], n_ctx)).astype(BF16)


def _gdn_prep(p, conv_a, cos_t, sin_t, n_ctx):
    b, l, _ = p.shape
    cb = OFF_AQKV // DK

    def pspec(off):
        return pl.BlockSpec((1, l, DK), lambda i, h: (i, 0, cb + off + h))

    def wspec(off):
        return pl.BlockSpec((CONV_K, DK), lambda i, h: (0, off + h))

    tab = pl.BlockSpec((l, DK), lambda i, h: (0, 0))
    out = pl.BlockSpec((1, l, DK), lambda i, h: (i, 0, h))
    shp = jax.ShapeDtypeStruct((b, l, H_A * DK), BF16)
    return pl.pallas_call(
        functools.partial(_gdn_prep_kernel, n_ctx),
        grid=(b, H_A),
        in_specs=[pspec(0), pspec(H_A), pspec(2 * H_A), wspec(0), wspec(H_A), wspec(2 * H_A), tab, tab],
        out_specs=[out, out, out],
        out_shape=[shp, shp, shp],
        compiler_params=_cparams("parallel", "parallel"),
        name="gdn_prep",
    )(p, p, p, conv_a, conv_a, conv_a, cos_t, sin_t)


def _gdn_gate_kernel(ps_ref, alog_ref, dt_ref, o_ref):
    x = ps_ref[0]
    l = x.shape[0]
    lane = lax.broadcasted_iota(jnp.int32, (1, N_SMALL), 1)
    g = -jnp.exp(alog_ref[...]) * _softplus(x + dt_ref[...])
    pos = lax.broadcasted_iota(jnp.int32, (l, 1), 0) % CHUNK
    fwd, bwd = g, g
    s = 1
    while s < CHUNK:
        fwd = fwd + jnp.where(pos >= s, pltpu.roll(fwd, s, 0), 0.0)
        bwd = bwd + jnp.where(pos < CHUNK - s, pltpu.roll(bwd, l - s, 0), 0.0)
        s *= 2
    gc = jnp.where(lane < 2 * H_A + H_A, fwd, bwd)
    o_ref[0] = jnp.where(lane < 2 * H_A, jax.nn.sigmoid(x), gc)


def _gdn_gates(psmall, alog_l, dt_l):
    b, l, n = psmall.shape
    vec = pl.BlockSpec((1, n), lambda i: (0, 0))
    return pl.pallas_call(
        _gdn_gate_kernel,
        grid=(b,),
        in_specs=[pl.BlockSpec((1, l, n), lambda i: (i, 0, 0)), vec, vec],
        out_specs=pl.BlockSpec((1, l, n), lambda i: (i, 0, 0)),
        out_shape=jax.ShapeDtypeStruct((b, l, n), F32),
        compiler_params=_cparams("parallel"),
        name="gdn_gates",
    )(psmall, alog_l, dt_l)


def _head_block_mask(shape, row_blk, col_blk):
    r = lax.broadcasted_iota(jnp.int32, shape, 0) // row_blk
    c = lax.broadcasted_iota(jnp.int32, shape, 1) // col_blk
    return r == c


def _block_diag_tile(x, reps, mask):
    return jnp.where(mask, jnp.concatenate([x] * reps, axis=0), jnp.zeros((), x.dtype))


def _dot_tn(a, b):
    return lax.dot_general(a, b, (((0,), (0,)), ((), ())), preferred_element_type=F32)


def _gdn_kernel(n_ctx, q_ref, k_ref, v_ref, gs_ref, gr_ref, o_ref, wv_ref, wk_ref, qk_ref, s_ref):
    l = q_ref.shape[1]
    n_chunks = l // CHUNK
    nc_ctx = n_ctx // CHUNK
    n_groups = n_chunks // PREP_CHUNKS
    wide = H_A * CHUNK
    n_levels = INV_BLK.bit_length() - 1
    ii = lax.broadcasted_iota(jnp.int32, (CHUNK, wide), 0)
    jw = lax.broadcasted_iota(jnp.int32, (CHUNK, wide), 1)
    jj, hw = jw % CHUNK, jw // CHUNK
    eye_w = jnp.where(ii == jj, 1.0, 0.0)
    m_kbd = _head_block_mask((wide, H_A * DK), CHUNK, DK)
    m_pbd = _head_block_mask((wide, wide), CHUNK, CHUNK)
    m_ubd = _head_block_mask((2 * CHUNK, 2 * DK), CHUNK, DK)
    m_sbd = _head_block_mask((2 * DK, 2 * DK), DK, DK)

    def chunk_rows(c):
        return pl.ds(pl.multiple_of(c * CHUNK, CHUNK), CHUNK)

    def step_chunk(i, d):
        return i if d == 0 else jnp.where(i < nc_ctx, nc_ctx - 1 - i, n_chunks + nc_ctx - 1 - i)

    def heads_to_rows(x):
        return jnp.concatenate([x[:, h * DK:(h + 1) * DK] for h in range(H_A)], axis=0)

    def splat_heads(cols):
        return jnp.concatenate([jnp.broadcast_to(c, (CHUNK, DK)) for c in cols], axis=-1)

    def bd(x):
        return _block_diag_tile(x, H_A, m_pbd)

    def prepare_stages(g):
        streams = []

        def load():
            for ci in range(PREP_CHUNKS):
                for d in range(2):
                    c = step_chunk(g * PREP_CHUNKS + ci, d)
                    rows = chunk_rows(c)
                    q_all, k_all, v_all = q_ref[0, rows, :], k_ref[0, rows, :], v_ref[0, rows, :]
                    gs = gs_ref[0, rows, :]
                    lane0 = N_CHAIN + d * H_A
                    beta = splat_heads([gs[:, d * H_A + h:d * H_A + h + 1] for h in range(H_A)])
                    e_gc = splat_heads([jnp.exp(gs[:, lane0 + h:lane0 + h + 1]) for h in range(H_A)])
                    kb = k_all.astype(F32) * beta
                    prod = _dot_nt(jnp.concatenate([kb.astype(BF16), q_all], axis=0),
                                   _block_diag_tile(k_all, H_A, m_kbd))
                    rhs = jnp.concatenate([heads_to_rows((v_all.astype(F32) * beta).astype(BF16)),
                                           heads_to_rows((kb * e_gc).astype(BF16))], axis=-1)
                    gc_w = jnp.broadcast_to(gs[:, lane0 + H_A - 1:lane0 + H_A], (CHUNK, wide))
                    for h in range(H_A - 2, -1, -1):
                        gc_w = jnp.where(hw == h, jnp.broadcast_to(gs[:, lane0 + h:lane0 + h + 1], (CHUNK, wide)),
                                         gc_w)
                    g_row = gr_ref[0, d, pl.ds(c, 1), :]
                    incl = (ii >= jj) if d == 0 else (ii <= jj)
                    strict = (ii > jj) if d == 0 else (ii < jj)
                    decay = jnp.where(incl, jnp.exp(jnp.where(incl, gc_w - g_row, 0.0)), 0.0)
                    a = jnp.where(strict, prod[:CHUNK] * decay, 0.0)
                    p = jnp.where(ii // INV_BLK == jj // INV_BLK, -a, 0.0)
                    streams.append(dict(d=d, rows=rows, a=a, p=p, t=eye_w + p, rhs=rhs,
                                        qk=(prod[CHUNK:] * decay).astype(BF16)))

        def square():
            for st in streams:
                pb = st["p"].astype(BF16)
                st["p"] = _dot(pb, bd(pb))

        def square_and_extend():
            for st in streams:
                pb = st["p"].astype(BF16)
                r = _dot(jnp.concatenate([pb, st["t"].astype(BF16)], axis=0), bd(pb))
                st["p"], st["t"] = r[:CHUNK], st["t"] + r[CHUNK:]

        def extend():
            for st in streams:
                st["t"] = st["t"] + _dot(st["t"].astype(BF16), bd(st["p"].astype(BF16)))

        def join_first(blk):
            def run():
                joins = (ii // (2 * blk) == jj // (2 * blk)) & (ii // blk != jj // blk)
                for st in streams:
                    st["tb"] = st["t"].astype(BF16)
                    st["w"] = _dot(jnp.where(joins, st["a"], 0.0).astype(BF16), bd(st["tb"]))
            return run

        def join_second():
            for st in streams:
                st["t"] = st["t"] - _dot(st["tb"], bd(st["w"].astype(BF16)))

        def solve():
            for st in streams:
                st["sol"] = _dot(bd(st["t"].astype(BF16)), st["rhs"])

        def store():
            for st in streams:
                d, rows = st["d"], st["rows"]
                qk_ref[d, rows, :] = st["qk"]
                for h in range(H_A):
                    sol_h = st["sol"][h * CHUNK:(h + 1) * CHUNK]
                    wv_ref[d, rows, h * DK:(h + 1) * DK] = sol_h[:, :DK].astype(BF16)
                    wk_ref[d, rows, h * DK:(h + 1) * DK] = sol_h[:, DK:].astype(BF16)

        stages = [load, square] + [square_and_extend] * (n_levels - 2) + [extend]
        blk = INV_BLK
        while blk < CHUNK:
            stages += [join_first(blk), join_second]
            blk *= 2
        return stages + [solve], store

    def recur_stages(g):
        stages = []
        for ci in range(PREP_CHUNKS):
            chains = []

            def start(ci=ci, chains=chains):
                for d in range(2):
                    c = step_chunk(g * PREP_CHUNKS + ci, d)
                    rows = chunk_rows(c)
                    gs = gs_ref[0, rows, :]
                    g_row = gr_ref[0, d, pl.ds(c, 1), :]
                    last = CHUNK - 1 if d == 0 else 0
                    for pr in range(H_A // 2):
                        hs = (2 * pr, 2 * pr + 1)
                        cols = slice(2 * pr * DK, (2 * pr + 2) * DK)
                        gcol = [gs[:, N_CHAIN + d * H_A + h:N_CHAIN + d * H_A + h + 1] for h in hs]
                        glast = [g_row[:, h * CHUNK + last:h * CHUNK + last + 1] for h in hs]
                        e_q = jnp.concatenate([jnp.broadcast_to(jnp.exp(gc), (CHUNK, DK)) for gc in gcol], axis=-1)
                        e_k = jnp.concatenate([jnp.broadcast_to(jnp.exp(gl - gc), (CHUNK, DK))
                                               for gc, gl in zip(gcol, glast)], axis=-1)
                        e_s = jnp.concatenate([jnp.broadcast_to(jnp.exp(gl), (1, DK)) for gl in glast], axis=-1)
                        q_dec = (q_ref[0, rows, cols].astype(F32) * e_q).astype(BF16)
                        k_tail = (k_ref[0, rows, cols].astype(F32) * e_k).astype(BF16)
                        s = s_ref[d, pr]
                        r = _dot(jnp.concatenate([wk_ref[d, rows, cols], q_dec], axis=0), s.astype(BF16))
                        chains.append(dict(d=d, pr=pr, rows=rows, cols=cols, e_s=e_s, k_tail=k_tail, s=s, r=r))

            def middle(chains=chains):
                for ch in chains:
                    d, pr, rows, cols = ch["d"], ch["pr"], ch["rows"], ch["cols"]
                    u_b = (wv_ref[d, rows, cols].astype(F32) - ch["r"][:CHUNK]).astype(BF16)
                    ch["ku"] = _dot_tn(ch["k_tail"], u_b)
                    ch["o"] = ch["r"][CHUNK:] + _dot(qk_ref[d, rows, 2 * pr * CHUNK:(2 * pr + 2) * CHUNK],
                                                     _block_diag_tile(u_b, 2, m_ubd))

            def finish(chains=chains):
                for ch in chains:
                    s_ref[ch["d"], ch["pr"]] = ch["s"] * ch["e_s"] + jnp.where(m_sbd, ch["ku"], 0.0)
                    o_ref[0, ch["rows"], ch["cols"]] += ch["o"]

            stages += [start, middle, finish]
        return stages

    def run_interleaved(first, second):
        for n in range(max(len(first), len(second))):
            if n < len(first):
                first[n]()
            if n < len(second):
                second[n]()

    s_ref[...] = jnp.zeros_like(s_ref)
    o_ref[...] = jnp.zeros_like(o_ref)

    stages, store = prepare_stages(0)
    run_interleaved(stages, [])
    store()

    def body(g, carry):
        stages, store = prepare_stages(g + 1)
        run_interleaved(stages, recur_stages(g))
        store()
        return carry

    lax.fori_loop(0, n_groups - 1, body, 0)
    run_interleaved([], recur_stages(n_groups - 1))


def _gdn(qn, kn, vn, gsm, g_rows, n_ctx):
    b, l, w = qn.shape
    n_chunks = l // CHUNK
    wide = H_A * CHUNK
    seq = pl.BlockSpec((1, l, w), lambda i: (i, 0, 0))
    return pl.pallas_call(
        functools.partial(_gdn_kernel, n_ctx),
        grid=(b,),
        in_specs=[seq, seq, seq, pl.BlockSpec((1, l, N_SMALL), lambda i: (i, 0, 0)),
                  pl.BlockSpec((1, 2, n_chunks, wide), lambda i: (i, 0, 0, 0))],
        out_specs=pl.BlockSpec((1, l, w), lambda i: (i, 0, 0)),
        out_shape=jax.ShapeDtypeStruct((b, l, w), F32),
        scratch_shapes=[pltpu.VMEM((2, l, w), BF16),
                        pltpu.VMEM((2, l, w), BF16),
                        pltpu.VMEM((2, l, wide), BF16),
                        pltpu.VMEM((2, H_A // 2, 2 * DK, 2 * DK), F32)],
        compiler_params=_cparams("parallel"),
        name="gdn",
    )(qn, kn, vn, gsm, g_rows)


def _lru_kernel(n_ctx, x_ref, cw_ref, cb_ref, w_ref, b_ref, lam_ref, o_ref, xc_ref, h_ref, hs_ref):
    l = x_ref.shape[1]
    n_blk = l // LRU_BLK
    nb_ctx = n_ctx // LRU_BLK
    wd = x_ref.shape[2]
    xc = _conv_centred(x_ref[0].astype(F32), cw_ref[...], n_ctx) + cb_ref[...]
    n_lg = wd // LANES
    for j in range(n_lg):
        xc_ref[j] = xc[:, j * LANES:(j + 1) * LANES]
    h_ref[...] = jnp.zeros_like(h_ref)
    o_ref[...] = jnp.zeros_like(o_ref)
    sub = lax.broadcasted_iota(jnp.int32, (SUBLANES, 1), 0)
    n_step = LRU_BLK // SUBLANES
    decay_rate = [(-RGLRU_C) * _softplus(-lam_ref[d]) for d in range(2)]

    def sigmoid(v):
        return 0.5 * jnp.tanh(0.5 * v) + 0.5

    def block(i, carry):
        c_bwd = jnp.where(i < nb_ctx, nb_ctx - 1 - i, n_blk + nb_ctx - 1 - i)
        for d in range(2):
            c = i if d == 0 else c_bwd
            base = pl.multiple_of(c * LRU_BLK, LRU_BLK)

            def step_rows(k):
                return pl.ds(base + k, SUBLANES, stride=n_step)

            x = jnp.concatenate([jnp.concatenate([xc_ref[j, step_rows(k), :] for j in range(n_lg)], axis=-1)
                                 for k in range(n_step)], axis=0)
            z = _dot(x.astype(BF16), w_ref[d]) + b_ref[d]
            log_a = sigmoid(z[:, :wd]) * decay_rate[d]
            a = jnp.exp(log_a)
            th = jnp.tanh(log_a)
            bb = jnp.sqrt(-2.0 * th / (1.0 - th)) * (sigmoid(z[:, wd:]) * x)
            order = list(range(n_step)) if d == 0 else list(range(n_step - 1, -1, -1))
            a_cum, b_cum = [None] * n_step, [None] * n_step
            for n, k in enumerate(order):
                sl = slice(k * SUBLANES, (k + 1) * SUBLANES)
                if n == 0:
                    a_cum[k], b_cum[k] = a[sl], bb[sl]
                else:
                    a_cum[k], b_cum[k] = a[sl] * a_cum[order[n - 1]], a[sl] * b_cum[order[n - 1]] + bb[sl]
            ga, gb = a_cum[order[-1]], b_cum[order[-1]]
            s = 1
            while s < SUBLANES:
                ok = (sub >= s) if d == 0 else (sub < SUBLANES - s)
                shift = s if d == 0 else SUBLANES - s
                gb = gb + ga * jnp.where(ok, pltpu.roll(gb, shift, 0), 0.0)
                ga = ga * jnp.where(ok, pltpu.roll(ga, shift, 0), 1.0)
                s *= 2
            h_in = jnp.broadcast_to(h_ref[d], (SUBLANES, wd))
            h_end = gb + ga * h_in
            if d == 0:
                h_grp = jnp.where(sub == 0, h_in, pltpu.roll(h_end, 1, 0))
                h_ref[d] = h_end[SUBLANES - 1:SUBLANES]
            else:
                h_grp = jnp.where(sub == SUBLANES - 1, h_in, pltpu.roll(h_end, SUBLANES - 1, 0))
                h_ref[d] = h_end[0:1]
            for k in range(n_step):
                h_k = b_cum[k] + a_cum[k] * h_grp
                for j in range(n_lg):
                    hs_ref[d, j, pl.ds(k, SUBLANES, stride=n_step), :] = h_k[:, j * LANES:(j + 1) * LANES]
            rows = pl.ds(base, LRU_BLK)
            o_ref[0, rows, :] += jnp.concatenate([hs_ref[d, j] for j in range(n_lg)], axis=-1)
        return carry

    lax.fori_loop(0, n_blk, block, 0)


def _lru(p, conv_b, conv_b_bias, w_bd, b_bd, lam, n_ctx):
    b, l, _ = p.shape
    return pl.pallas_call(
        functools.partial(_lru_kernel, n_ctx),
        grid=(b,),
        in_specs=[pl.BlockSpec((1, l, W_BR), lambda i: (i, 0, OFF_BX // W_BR)),
                  pl.BlockSpec((CONV_K, W_BR), lambda i: (0, 0)),
                  pl.BlockSpec((1, W_BR), lambda i: (0, 0)),
                  pl.BlockSpec((2, W_BR, 2 * W_BR), lambda i: (0, 0, 0)),
                  pl.BlockSpec((2, 1, 2 * W_BR), lambda i: (0, 0, 0)),
                  pl.BlockSpec((2, 1, W_BR), lambda i: (0, 0, 0))],
        out_specs=pl.BlockSpec((1, l, W_BR), lambda i: (i, 0, 0)),
        out_shape=jax.ShapeDtypeStruct((b, l, W_BR), F32),
        scratch_shapes=[pltpu.VMEM((W_BR // LANES, l, LANES), F32), pltpu.VMEM((2, 1, W_BR), F32),
                        pltpu.VMEM((2, W_BR // LANES, LRU_BLK, LANES), F32)],
        compiler_params=_cparams("parallel"),
        name="rglru",
    )(p, conv_b, conv_b_bias, w_bd, b_bd, lam)


def _attn_kernel(n_ctx, rows, q_ref, k_ref, v_ref, bias_ref, o_ref):
    scale = DK ** -0.5
    n_blk = rows // Q_ROWS
    nq = Q_ROWS * GRID_W
    nk = K_ROWS * GRID_W
    kc = k_ref[0, :n_ctx, :]
    vc = v_ref[0, :n_ctx, :]

    blocks = [(slice(0, n_ctx), [(None, None)])]
    for j in range(n_blk):
        kb = min(max(Q_ROWS * j - WIN_R // 2, 0), rows - K_ROWS)
        var = 0 if j == 0 else (2 if j == n_blk - 1 else 1)
        k0 = n_ctx + kb * GRID_W
        blocks.append((slice(n_ctx + j * nq, n_ctx + (j + 1) * nq), [(slice(k0, k0 + nk), var), (None, None)]))

    def scores(block):
        q_rows, parts = block
        q = (q_ref[0, q_rows, :].astype(F32) * scale).astype(BF16)
        ss = []
        for k_rows, var in parts:
            s = _dot_nt(q, kc if k_rows is None else k_ref[0, k_rows, :])
            ss.append(s if var is None else s + bias_ref[0, var])
        return ss

    def finish(block, ss):
        q_rows, parts = block
        m = functools.reduce(jnp.maximum, [jnp.max(s, axis=-1, keepdims=True) for s in ss])
        ps = [jnp.exp(s - m) for s in ss]
        den = functools.reduce(jnp.add, [jnp.sum(p, axis=-1, keepdims=True) for p in ps])
        num = functools.reduce(jnp.add, [_dot(p.astype(BF16), vc if k_rows is None else v_ref[0, k_rows, :])
                                         for p, (k_rows, _) in zip(ps, parts)])
        o_ref[0, q_rows, :] = num / den

    pending = scores(blocks[0])
    for n, block in enumerate(blocks):
        upcoming = scores(blocks[n + 1]) if n + 1 < len(blocks) else None
        finish(block, pending)
        pending = upcoming


def _attention(p, bias, n_ctx):
    b, l, _ = p.shape
    rows = (l - n_ctx) // GRID_W
    cb = OFF_CQKV // DK

    def pspec(off):
        return pl.BlockSpec((1, l, DK), lambda h, i: (i, 0, cb + off + h))

    return pl.pallas_call(
        functools.partial(_attn_kernel, n_ctx, rows),
        grid=(H_C, b),
        in_specs=[pspec(0), pspec(H_C), pspec(2 * H_C),
                  pl.BlockSpec((1,) + bias.shape[1:], lambda h, i: (h, 0, 0, 0))],
        out_specs=pl.BlockSpec((1, l, DK), lambda h, i: (i, 0, h)),
        out_shape=jax.ShapeDtypeStruct((b, l, H_C * DK), F32),
        compiler_params=_cparams("parallel", "parallel"),
        name="natten",
    )(p, p, p, bias)


def _attention_bias(rpb, rows):
    n_blk = rows // Q_ROWS
    qc = np.arange(GRID_W)[:, None]
    kc = np.arange(GRID_W)[None, :]
    c0 = np.clip(qc - WIN_C // 2, 0, GRID_W - WIN_C)
    col_ok = (kc >= c0) & (kc < c0 + WIN_C)
    dc = np.clip(kc - qc, -(WIN_C - 1), WIN_C - 1) + WIN_C - 1
    onehot = (dc[None] == np.arange(2 * WIN_C - 1)[:, None, None]).astype(np.float32)
    cols = jnp.einsum("...rd,dqk->...rqk", rpb.astype(F32), onehot, precision=lax.Precision.HIGHEST)
    cols = jnp.where(col_ok, cols, NEG)
    masked = jnp.full(cols.shape[:-3] + (GRID_W, GRID_W), NEG, F32)
    variants = []
    for j in (0, 1, n_blk - 1):
        kb = min(max(Q_ROWS * j - WIN_R // 2, 0), rows - K_ROWS)
        q_blocks = []
        for a in range(Q_ROWS):
            qr = Q_ROWS * j + a
            r0 = min(max(qr - WIN_R // 2, 0), rows - WIN_R)
            k_blocks = []
            for b in range(K_ROWS):
                kr = kb + b
                k_blocks.append(cols[..., kr - qr + WIN_R - 1, :, :] if r0 <= kr < r0 + WIN_R else masked)
            q_blocks.append(jnp.concatenate(k_blocks, axis=-1))
        variants.append(jnp.concatenate(q_blocks, axis=-2))
    return jnp.stack(variants, axis=-3)


def _merge_kernel(ctx_tiles, tile_off, oa_ref, hb_ref, oc_ref, ga_ref, gb_ref, gc_ref, m0_ref, m1_ref, m2_ref,
                  on_ref, wb_ref, wo_ref, x_ref, gp_ref, gl_ref, gcx_ref, o_ref):
    oa = oa_ref[0]
    parts = []
    for h in range(H_A):
        xh = oa[:, h * DK:(h + 1) * DK]
        parts.append(xh * lax.rsqrt(jnp.mean(xh * xh, axis=-1, keepdims=True) + EPS) * on_ref[...])
    ya = jnp.concatenate(parts, axis=-1) * _silu(ga_ref[0].astype(F32))
    yb = hb_ref[0] * _silu(gb_ref[0].astype(F32))
    yc = oc_ref[0] * _silu(gc_ref[0].astype(F32))
    merged = (jax.nn.sigmoid(m0_ref[0].astype(F32)) * _dot(ya.astype(BF16), wb_ref[0])
              + jax.nn.sigmoid(m1_ref[0].astype(F32)) * _dot(yb.astype(BF16), wb_ref[1])
              + jax.nn.sigmoid(m2_ref[0].astype(F32)) * _dot(yc.astype(BF16), wb_ref[2]))
    y = _dot(merged.astype(BF16), wo_ref[...])
    yn = y * lax.rsqrt(jnp.mean(y * y, axis=-1, keepdims=True) + EPS) * gp_ref[...]
    is_ctx = (pl.program_id(1) + tile_off) < ctx_tiles
    gate = jnp.where(is_ctx, gcx_ref[...], gl_ref[0])
    o_ref[0] = x_ref[0] + gate * yn


def _merge(oa, hb, oc, p, onorm, w_branch, w_out, xs, g_post, gate_lat, gate_ctx, n_ctx, lat_only):
    b, l, d = xs.shape
    tm = 256
    ctx_tiles = n_ctx // tm
    tile_off = ctx_tiles if lat_only else 0
    n_tiles = l // tm - tile_off

    def tok(width, col=0):
        return pl.BlockSpec((1, tm, width), lambda i, t: (i, t + tile_off, col))

    def const(shape):
        return pl.BlockSpec(shape, lambda i, t: (0,) * len(shape))

    return pl.pallas_call(
        functools.partial(_merge_kernel, ctx_tiles, tile_off),
        grid=(b, n_tiles),
        in_specs=[tok(W_BR), tok(W_BR), tok(W_BR),
                  tok(W_BR, OFF_AGATE // W_BR), tok(W_BR, OFF_BGATE // W_BR), tok(W_BR, OFF_CGATE // W_BR),
                  tok(d, OFF_MG // d), tok(d, OFF_MG // d + 1), tok(d, OFF_MG // d + 2),
                  const((1, DK)), const((3, W_BR, d)), const((d, d)),
                  tok(d), const((1, d)),
                  pl.BlockSpec((1, 1, d), lambda i, t: (i, 0, 0)), const((1, d))],
        out_specs=pl.BlockSpec((1, tm, d), lambda i, t: (i, t, 0)),
        out_shape=jax.ShapeDtypeStruct((b, n_tiles * tm, d), F32),
        compiler_params=_cparams("parallel", "parallel"),
        name="merge",
    )(oa, hb, oc, p, p, p, p, p, p, onorm, w_branch, w_out, xs, g_post, gate_lat, gate_ctx)


def _rope_tables(n_ctx, s_len):
    quarter = DK // 4
    pos = jnp.arange(s_len)
    inv_freq = ROPE_BASE ** (-jnp.arange(quarter, dtype=F32) / quarter)
    ang_r = (pos // GRID_W).astype(F32)[:, None] * inv_freq
    ang_c = (pos % GRID_W).astype(F32)[:, None] * inv_freq
    cos = jnp.concatenate([jnp.cos(ang_r)] * 2 + [jnp.cos(ang_c)] * 2, axis=-1)
    sin = jnp.concatenate([-jnp.sin(ang_r), jnp.sin(ang_r), -jnp.sin(ang_c), jnp.sin(ang_c)], axis=-1)
    cos = jnp.concatenate([jnp.ones((n_ctx, DK), F32), cos], axis=0)
    sin = jnp.concatenate([jnp.zeros((n_ctx, DK), F32), sin], axis=0)
    return cos, sin


def _block_diag(w):
    nb, bw, _ = w.shape
    return jnp.einsum("nde,nm->ndme", w, jnp.eye(nb, dtype=w.dtype)).reshape(nb * bw, nb * bw)


def kernel(x, c, ctx, c_ctx, w_mod, b_mod, g_pre, g_post, w_in, conv_a, a_log, dt_bias, onorm_a, conv_b,
           conv_b_bias, lru_wa, lru_ba, lru_wx, lru_bx, lru_lam, rpb, w_branch, w_out):
    bsz, s_len, d = x.shape
    n_ctx = ctx.shape[1]
    depth = w_in.shape[0]
    l = n_ctx + s_len
    n_chunks = l // CHUNK
    rows = s_len // GRID_W
    assert d == 1024 and n_ctx % 256 == 0 and s_len % (Q_ROWS * GRID_W) == 0 and rows >= K_ROWS
    assert n_chunks % PREP_CHUNKS == 0 and (n_ctx // CHUNK) % PREP_CHUNKS == 0

    pad = (-(bsz + 1)) % SUBLANES
    cvec = jnp.concatenate([c, c_ctx[None], jnp.zeros((pad, d), F32)], axis=0)
    mod = _modulation(cvec, w_mod, b_mod)
    cos_t, sin_t = _rope_tables(n_ctx, s_len)
    attn_bias = _attention_bias(rpb, rows)

    sm0 = 4 * W_BR
    w_main = jnp.concatenate([w_in[:, :, :sm0], w_in[:, :, sm0 + 2 * N_CHAIN:]], axis=-1).astype(BF16)
    w_small = jnp.pad(w_in[:, :, sm0:sm0 + 2 * N_CHAIN], ((0, 0), (0, 0), (0, N_SMALL - 2 * N_CHAIN))).astype(BF16)
    lane_pad = ((0, 0), (N_CHAIN, N_SMALL - 2 * N_CHAIN))
    alog_l = jnp.pad(a_log.reshape(depth, N_CHAIN), lane_pad)
    dt_l = jnp.pad(dt_bias.reshape(depth, N_CHAIN), lane_pad)
    w_br = w_branch.astype(BF16)
    w_o = w_out.astype(BF16)

    xs = jnp.concatenate([ctx, x], axis=1)
    for li in range(depth):
        last = li == depth - 1
        mod_lat = mod[li, :bsz, None, :]
        mod_ctx = mod[li, bsz:bsz + 1]
        p, psmall = _in_proj(xs, mod_lat, mod_ctx, g_pre[li][None], w_main[li], w_small[li], n_ctx)

        qn, kn, vn = _gdn_prep(p, conv_a[li], cos_t, sin_t, n_ctx)
        gsm = _gdn_gates(psmall, alog_l[li][None], dt_l[li][None])
        def rows_of(t):
            t = t.reshape(bsz, n_chunks, CHUNK, 2, H_A).transpose(0, 3, 1, 4, 2)
            return t.reshape(bsz, 2, n_chunks, H_A * CHUNK)

        oa = _gdn(qn, kn, vn, gsm, rows_of(gsm[:, :, N_CHAIN:2 * N_CHAIN]), n_ctx)

        w_bd = jnp.stack([jnp.concatenate([_block_diag(lru_wa[li, dd]), _block_diag(lru_wx[li, dd])], axis=-1)
                          for dd in range(2)]).astype(BF16)
        b_bd = jnp.concatenate([lru_ba[li], lru_bx[li]], axis=-1)[:, None, :]
        hb = _lru(p, conv_b[li], conv_b_bias[li][None], w_bd, b_bd, lru_lam[li][:, None, :], n_ctx)

        oc = _attention(p, attn_bias[li], n_ctx)

        xs = _merge(oa, hb, oc, p, onorm_a[li][None], w_br[li], w_o[li], xs, g_post[li][None],
                    mod[li, :bsz, None, 2 * d:], mod[li, bsz:bsz + 1, 2 * d:], n_ctx, last)
    return xs
```

```python
import functools
import math

import numpy as np
import jax
import jax.numpy as jnp
from jax import lax
from jax.experimental import pallas as pl
from jax.experimental.pallas import tpu as pltpu

F32 = jnp.float32
BF16 = jnp.bfloat16

EPS = 1e-6
GRID_W = 64
CONV_K = 4
H_A = 4
DK = 128
CHUNK = 64
N_CHAIN = 2 * H_A
INV_BLK = 16
PREP_CHUNKS = 4
W_BR = 512
NB_B = 8
RGLRU_C = 8.0
H_C = 4
WIN_R = 8
WIN_C = 16
ROPE_BASE = 10000.0
Q_ROWS = 4
K_ROWS = 12
NEG = -1e30
LRU_BLK = 64
SUBLANES = 8
LANES = 128
VMEM_LIMIT = 56 * 1024 * 1024

OFF_AQKV, OFF_AGATE, OFF_BX, OFF_BGATE, OFF_CQKV, OFF_CGATE, OFF_MG = 0, 1536, 2048, 2560, 3072, 4608, 5120
N_MAIN = 8192
N_SMALL = 128


def _cparams(*sem):
    return pltpu.CompilerParams(dimension_semantics=sem, vmem_limit_bytes=VMEM_LIMIT)


def _dot(a, b):
    return jnp.dot(a, b, preferred_element_type=F32)


def _dot_nt(a, b):
    return lax.dot_general(a, b, (((1,), (1,)), ((), ())), preferred_element_type=F32)


def _sigmoid(x):
    return 0.5 * jnp.tanh(0.5 * x) + 0.5


def _silu(x):
    half = 0.5 * x
    return half * jnp.tanh(half) + half


def _softplus(x):
    return jnp.maximum(x, 0.0) + jnp.log1p(jnp.exp(-jnp.abs(x)))


def _mod_kernel(c_ref, w_ref, b_ref, o_ref):
    o_ref[0] = _dot(_silu(c_ref[...]).astype(BF16), w_ref[0].astype(BF16)) + b_ref[0]


def _modulation(cvec, w_mod, b_mod):
    depth, d, n = w_mod.shape
    rows = cvec.shape[0]
    tn = 768
    return pl.pallas_call(
        _mod_kernel,
        grid=(depth, n // tn),
        in_specs=[pl.BlockSpec((rows, d), lambda l, j: (0, 0)),
                  pl.BlockSpec((1, d, tn), lambda l, j: (l, 0, j)),
                  pl.BlockSpec((1, 1, tn), lambda l, j: (l, 0, j))],
        out_specs=pl.BlockSpec((1, rows, tn), lambda l, j: (l, 0, j)),
        out_shape=jax.ShapeDtypeStruct((depth, rows, n), F32),
        compiler_params=_cparams("parallel", "parallel"),
        name="modulation",
    )(cvec, w_mod, b_mod.reshape(depth, 1, n))


def _inproj_kernel(n_ctx, x_ref, ml_ref, mc_ref, g_ref, w_ref, ws_ref, o_ref, os_ref, h_ref):
    d = x_ref.shape[-1]

    @pl.when(pl.program_id(1) == 0)
    def _():
        g = g_ref[...]

        def normed(x, mod):
            y = x * lax.rsqrt(jnp.mean(x * x, axis=-1, keepdims=True) + EPS) * g
            return (y * (1.0 + mod[:, d:2 * d]) + mod[:, :d]).astype(BF16)

        h_ref[:n_ctx] = normed(x_ref[0, :n_ctx], mc_ref[...])
        h_ref[n_ctx:] = normed(x_ref[0, n_ctx:], ml_ref[0])
        os_ref[0] = _dot(h_ref[...], ws_ref[...])

    o_ref[0] = _dot(h_ref[...], w_ref[...]).astype(o_ref.dtype)


def _in_proj(xs, mod_lat, mod_ctx, g_pre, w_main, w_small, n_ctx):
    b, l, d = xs.shape
    tn = 1024
    return pl.pallas_call(
        functools.partial(_inproj_kernel, n_ctx),
        grid=(b, N_MAIN // tn),
        in_specs=[pl.BlockSpec((1, l, d), lambda i, j: (i, 0, 0)),
                  pl.BlockSpec((1, 1, 3 * d), lambda i, j: (i, 0, 0)),
                  pl.BlockSpec((1, 3 * d), lambda i, j: (0, 0)),
                  pl.BlockSpec((1, d), lambda i, j: (0, 0)),
                  pl.BlockSpec((d, tn), lambda i, j: (0, j)),
                  pl.BlockSpec((d, N_SMALL), lambda i, j: (0, 0))],
        out_specs=[pl.BlockSpec((1, l, tn), lambda i, j: (i, 0, j)),
                   pl.BlockSpec((1, l, N_SMALL), lambda i, j: (i, 0, 0))],
        out_shape=[jax.ShapeDtypeStruct((b, l, N_MAIN), BF16),
                   jax.ShapeDtypeStruct((b, l, N_SMALL), F32)],
        scratch_shapes=[pltpu.VMEM((l, d), BF16)],
        compiler_params=_cparams("parallel", "arbitrary"),
        name="in_proj",
    )(xs, mod_lat, mod_ctx, g_pre, w_main, w_small)


def _conv_centred(x, w, n_ctx):
    l = x.shape[0]
    t = lax.broadcasted_iota(jnp.int32, (l, 1), 0)
    tl = jnp.where(t < n_ctx, t, t - n_ctx)
    rem = jnp.where(t < n_ctx, n_ctx - 1 - t, l - 1 - t)
    acc = x * w[CONV_K // 2:CONV_K // 2 + 1]
    for k in range(CONV_K):
        off = k - CONV_K // 2
        if off == 0:
            continue
        shifted = pltpu.roll(x, (-off) % l, 0)
        ok = (tl >= -off) if off < 0 else (rem >= off)
        acc = acc + jnp.where(ok, shifted, 0.0) * w[k:k + 1]
    return acc


def _gdn_prep_kernel(n_ctx, pq_ref, pk_ref, pv_ref, wq_ref, wk_ref, wv_ref, cos_ref, sin_ref,
                     q_ref, k_ref, v_ref):
    lane = lax.broadcasted_iota(jnp.int32, (1, DK), 1)
    first = (lane % (DK // 2)) < (DK // 4)

    def qk_path(p_ref, w_ref, scale):
        x = _silu(_conv_centred(p_ref[0].astype(F32), w_ref[...], n_ctx))
        x = x * lax.rsqrt(jnp.sum(x * x, axis=-1, keepdims=True) + EPS)
        partner = jnp.where(first, pltpu.roll(x, DK - DK // 4, 1), pltpu.roll(x, DK // 4, 1))
        x = x * cos_ref[...] + partner * sin_ref[...]
        if scale != 1.0:
            x = x * scale
        return x.astype(BF16)

    q_ref[0] = qk_path(pq_ref, wq_ref, DK ** -0.5)
    k_ref[0] = qk_path(pk_ref, wk_ref, 1.0)
    v_ref[0] = _silu(_conv_centred(pv_ref[0].astype(F32), wv_ref[...], n_ctx)).astype(BF16)


def _gdn_prep(p, conv_a, cos_t, sin_t, n_ctx):
    b, l, _ = p.shape
    cb = OFF_AQKV // DK

    def pspec(off):
        return pl.BlockSpec((1, l, DK), lambda i, h: (i, 0, cb + off + h))

    def wspec(off):
        return pl.BlockSpec((CONV_K, DK), lambda i, h: (0, off + h))

    tab = pl.BlockSpec((l, DK), lambda i, h: (0, 0))
    out = pl.BlockSpec((1, l, DK), lambda i, h: (i, 0, h))
    shp = jax.ShapeDtypeStruct((b, l, H_A * DK), BF16)
    return pl.pallas_call(
        functools.partial(_gdn_prep_kernel, n_ctx),
        grid=(b, H_A),
        in_specs=[pspec(0), pspec(H_A), pspec(2 * H_A), wspec(0), wspec(H_A), wspec(2 * H_A), tab, tab],
        out_specs=[out, out, out],
        out_shape=[shp, shp, shp],
        compiler_params=_cparams("parallel", "parallel"),
        name="gdn_prep",
    )(p, p, p, conv_a, conv_a, conv_a, cos_t, sin_t)


def _gdn_gate_kernel(ps_ref, alog_ref, dt_ref, o_ref):
    x = ps_ref[0]
    l = x.shape[0]
    lane = lax.broadcasted_iota(jnp.int32, (1, N_SMALL), 1)
    g = -jnp.exp(alog_ref[...]) * _softplus(x + dt_ref[...])
    pos = lax.broadcasted_iota(jnp.int32, (l, 1), 0) % CHUNK
    fwd, bwd = g, g
    s = 1
    while s < CHUNK:
        fwd = fwd + jnp.where(pos >= s, pltpu.roll(fwd, s, 0), 0.0)
        bwd = bwd + jnp.where(pos < CHUNK - s, pltpu.roll(bwd, l - s, 0), 0.0)
        s *= 2
    gc = jnp.where(lane < 2 * H_A + H_A, fwd, bwd)
    o_ref[0] = jnp.where(lane < 2 * H_A, _sigmoid(x), gc)


def _gdn_gates(psmall, alog_l, dt_l):
    b, l, n = psmall.shape
    vec = pl.BlockSpec((1, n), lambda i: (0, 0))
    return pl.pallas_call(
        _gdn_gate_kernel,
        grid=(b,),
        in_specs=[pl.BlockSpec((1, l, n), lambda i: (i, 0, 0)), vec, vec],
        out_specs=pl.BlockSpec((1, l, n), lambda i: (i, 0, 0)),
        out_shape=jax.ShapeDtypeStruct((b, l, n), F32),
        compiler_params=_cparams("parallel"),
        name="gdn_gates",
    )(psmall, alog_l, dt_l)


def _head_block_mask(shape, row_blk, col_blk):
    r = lax.broadcasted_iota(jnp.int32, shape, 0) // row_blk
    c = lax.broadcasted_iota(jnp.int32, shape, 1) // col_blk
    return r == c


def _block_diag_tile(x, reps, mask):
    return jnp.where(mask, jnp.concatenate([x] * reps, axis=0), jnp.zeros((), x.dtype))


def _dot_tn(a, b):
    return lax.dot_general(a, b, (((0,), (0,)), ((), ())), preferred_element_type=F32)


def _gdn_kernel(n_ctx, q_ref, k_ref, v_ref, gs_ref, gr_ref, o_ref, wv_ref, wk_ref, qk_ref, s_ref):
    l = q_ref.shape[1]
    n_chunks = l // CHUNK
    nc_ctx = n_ctx // CHUNK
    n_groups = n_chunks // PREP_CHUNKS
    wide = H_A * CHUNK
    n_levels = INV_BLK.bit_length() - 1
    ii = lax.broadcasted_iota(jnp.int32, (CHUNK, wide), 0)
    jw = lax.broadcasted_iota(jnp.int32, (CHUNK, wide), 1)
    jj, hw = jw % CHUNK, jw // CHUNK
    eye_w = jnp.where(ii == jj, 1.0, 0.0)
    m_kbd = _head_block_mask((wide, H_A * DK), CHUNK, DK)
    m_pbd = _head_block_mask((wide, wide), CHUNK, CHUNK)
    m_ubd = _head_block_mask((2 * CHUNK, 2 * DK), CHUNK, DK)
    m_sbd = _head_block_mask((2 * DK, 2 * DK), DK, DK)

    def chunk_rows(c):
        return pl.ds(pl.multiple_of(c * CHUNK, CHUNK), CHUNK)

    def step_chunk(i, d):
        return i if d == 0 else jnp.where(i < nc_ctx, nc_ctx - 1 - i, n_chunks + nc_ctx - 1 - i)

    def heads_to_rows(x):
        return jnp.concatenate([x[:, h * DK:(h + 1) * DK] for h in range(H_A)], axis=0)

    def splat_heads(cols):
        return jnp.concatenate([jnp.broadcast_to(c, (CHUNK, DK)) for c in cols], axis=-1)

    def bd(x):
        return _block_diag_tile(x, H_A, m_pbd)

    def prepare_stages(g):
        streams = []

        def load():
            for ci in range(PREP_CHUNKS):
                for d in range(2):
                    c = step_chunk(g * PREP_CHUNKS + ci, d)
                    rows = chunk_rows(c)
                    q_all, k_all, v_all = q_ref[0, rows, :], k_ref[0, rows, :], v_ref[0, rows, :]
                    gs = gs_ref[0, rows, :]
                    lane0 = N_CHAIN + d * H_A
                    beta = splat_heads([gs[:, d * H_A + h:d * H_A + h + 1] for h in range(H_A)])
                    e_gc = splat_heads([jnp.exp(gs[:, lane0 + h:lane0 + h + 1]) for h in range(H_A)])
                    kb = k_all.astype(F32) * beta
                    prod = _dot_nt(jnp.concatenate([kb.astype(BF16), q_all], axis=0),
                                   _block_diag_tile(k_all, H_A, m_kbd))
                    rhs = jnp.concatenate([heads_to_rows((v_all.astype(F32) * beta).astype(BF16)),
                                           heads_to_rows((kb * e_gc).astype(BF16))], axis=-1)
                    gc_w = jnp.broadcast_to(gs[:, lane0 + H_A - 1:lane0 + H_A], (CHUNK, wide))
                    for h in range(H_A - 2, -1, -1):
                        gc_w = jnp.where(hw == h, jnp.broadcast_to(gs[:, lane0 + h:lane0 + h + 1], (CHUNK, wide)),
                                         gc_w)
                    g_row = gr_ref[0, d, pl.ds(c, 1), :]
                    incl = (ii >= jj) if d == 0 else (ii <= jj)
                    strict = (ii > jj) if d == 0 else (ii < jj)
                    decay = jnp.where(incl, jnp.exp(jnp.where(incl, gc_w - g_row, 0.0)), 0.0)
                    a = jnp.where(strict, prod[:CHUNK] * decay, 0.0)
                    p = jnp.where(ii // INV_BLK == jj // INV_BLK, -a, 0.0)
                    streams.append(dict(d=d, rows=rows, a=a, p=p, t=eye_w + p, rhs=rhs,
                                        qk=(prod[CHUNK:] * decay).astype(BF16)))

        def square():
            for st in streams:
                pb = st["p"].astype(BF16)
                st["p"] = _dot(pb, bd(pb))

        def square_and_extend():
            for st in streams:
                pb = st["p"].astype(BF16)
                r = _dot(jnp.concatenate([pb, st["t"].astype(BF16)], axis=0), bd(pb))
                st["p"], st["t"] = r[:CHUNK], st["t"] + r[CHUNK:]

        def extend():
            for st in streams:
                st["t"] = st["t"] + _dot(st["t"].astype(BF16), bd(st["p"].astype(BF16)))

        def join_first(blk):
            def run():
                joins = (ii // (2 * blk) == jj // (2 * blk)) & (ii // blk != jj // blk)
                for st in streams:
                    st["tb"] = st["t"].astype(BF16)
                    st["w"] = _dot(jnp.where(joins, st["a"], 0.0).astype(BF16), bd(st["tb"]))
            return run

        def join_second():
            for st in streams:
                st["t"] = st["t"] - _dot(st["tb"], bd(st["w"].astype(BF16)))

        def solve():
            for st in streams:
                st["sol"] = _dot(bd(st["t"].astype(BF16)), st["rhs"])

        def store():
            for st in streams:
                d, rows = st["d"], st["rows"]
                qk_ref[d, rows, :] = st["qk"]
                for h in range(H_A):
                    sol_h = st["sol"][h * CHUNK:(h + 1) * CHUNK]
                    wv_ref[d, rows, h * DK:(h + 1) * DK] = sol_h[:, :DK].astype(BF16)
                    wk_ref[d, rows, h * DK:(h + 1) * DK] = sol_h[:, DK:].astype(BF16)

        stages = [load, square] + [square_and_extend] * (n_levels - 2) + [extend]
        blk = INV_BLK
        while blk < CHUNK:
            stages += [join_first(blk), join_second]
            blk *= 2
        return stages + [solve], store

    def recur_stages(g):
        stages = []
        for ci in range(PREP_CHUNKS):
            chains = []

            def start(ci=ci, chains=chains):
                for d in range(2):
                    c = step_chunk(g * PREP_CHUNKS + ci, d)
                    rows = chunk_rows(c)
                    gs = gs_ref[0, rows, :]
                    g_row = gr_ref[0, d, pl.ds(c, 1), :]
                    last = CHUNK - 1 if d == 0 else 0
                    for pr in range(H_A // 2):
                        hs = (2 * pr, 2 * pr + 1)
                        cols = slice(2 * pr * DK, (2 * pr + 2) * DK)
                        gcol = [gs[:, N_CHAIN + d * H_A + h:N_CHAIN + d * H_A + h + 1] for h in hs]
                        glast = [g_row[:, h * CHUNK + last:h * CHUNK + last + 1] for h in hs]
                        e_q = jnp.concatenate([jnp.broadcast_to(jnp.exp(gc), (CHUNK, DK)) for gc in gcol], axis=-1)
                        e_k = jnp.concatenate([jnp.broadcast_to(jnp.exp(gl - gc), (CHUNK, DK))
                                               for gc, gl in zip(gcol, glast)], axis=-1)
                        e_s = jnp.concatenate([jnp.broadcast_to(jnp.exp(gl), (1, DK)) for gl in glast], axis=-1)
                        q_dec = (q_ref[0, rows, cols].astype(F32) * e_q).astype(BF16)
                        k_tail = (k_ref[0, rows, cols].astype(F32) * e_k).astype(BF16)
                        s = s_ref[d, pr]
                        r = _dot(jnp.concatenate([wk_ref[d, rows, cols], q_dec], axis=0), s.astype(BF16))
                        chains.append(dict(d=d, pr=pr, rows=rows, cols=cols, e_s=e_s, k_tail=k_tail, s=s, r=r))

            def middle(chains=chains):
                for ch in chains:
                    d, pr, rows, cols = ch["d"], ch["pr"], ch["rows"], ch["cols"]
                    u_b = (wv_ref[d, rows, cols].astype(F32) - ch["r"][:CHUNK]).astype(BF16)
                    ch["ku"] = _dot_tn(ch["k_tail"], u_b)
                    ch["o"] = ch["r"][CHUNK:] + _dot(qk_ref[d, rows, 2 * pr * CHUNK:(2 * pr + 2) * CHUNK],
                                                     _block_diag_tile(u_b, 2, m_ubd))

            def finish(chains=chains):
                for ch in chains:
                    s_ref[ch["d"], ch["pr"]] = ch["s"] * ch["e_s"] + jnp.where(m_sbd, ch["ku"], 0.0)
                    o_ref[0, ch["rows"], ch["cols"]] += ch["o"]

            stages += [start, middle, finish]
        return stages

    def run_interleaved(first, second):
        for n in range(max(len(first), len(second))):
            if n < len(first):
                first[n]()
            if n < len(second):
                second[n]()

    s_ref[...] = jnp.zeros_like(s_ref)
    o_ref[...] = jnp.zeros_like(o_ref)

    stages, store = prepare_stages(0)
    run_interleaved(stages, [])
    store()

    def body(g, carry):
        stages, store = prepare_stages(g + 1)
        run_interleaved(stages, recur_stages(g))
        store()
        return carry

    lax.fori_loop(0, n_groups - 1, body, 0)
    run_interleaved([], recur_stages(n_groups - 1))


def _gdn(qn, kn, vn, gsm, g_rows, n_ctx):
    b, l, w = qn.shape
    n_chunks = l // CHUNK
    wide = H_A * CHUNK
    seq = pl.BlockSpec((1, l, w), lambda i: (i, 0, 0))
    return pl.pallas_call(
        functools.partial(_gdn_kernel, n_ctx),
        grid=(b,),
        in_specs=[seq, seq, seq, pl.BlockSpec((1, l, N_SMALL), lambda i: (i, 0, 0)),
                  pl.BlockSpec((1, 2, n_chunks, wide), lambda i: (i, 0, 0, 0))],
        out_specs=pl.BlockSpec((1, l, w), lambda i: (i, 0, 0)),
        out_shape=jax.ShapeDtypeStruct((b, l, w), F32),
        scratch_shapes=[pltpu.VMEM((2, l, w), BF16),
                        pltpu.VMEM((2, l, w), BF16),
                        pltpu.VMEM((2, l, wide), BF16),
                        pltpu.VMEM((2, H_A // 2, 2 * DK, 2 * DK), F32)],
        compiler_params=_cparams("parallel"),
        name="gdn",
    )(qn, kn, vn, gsm, g_rows)


def _lru_kernel(n_ctx, x_ref, cw_ref, cb_ref, w_ref, b_ref, lam_ref, o_ref, xc_ref, h_ref, hs_ref):
    l = x_ref.shape[1]
    n_blk = l // LRU_BLK
    nb_ctx = n_ctx // LRU_BLK
    wd = x_ref.shape[2]
    xc = _conv_centred(x_ref[0].astype(F32), cw_ref[...], n_ctx) + cb_ref[...]
    n_lg = wd // LANES
    for j in range(n_lg):
        xc_ref[j] = xc[:, j * LANES:(j + 1) * LANES]
    h_ref[...] = jnp.zeros_like(h_ref)
    o_ref[...] = jnp.zeros_like(o_ref)
    sub = lax.broadcasted_iota(jnp.int32, (SUBLANES, 1), 0)
    n_step = LRU_BLK // SUBLANES
    decay_rate = [(-RGLRU_C) * _softplus(-lam_ref[d]) for d in range(2)]

    def block(i, carry):
        c_bwd = jnp.where(i < nb_ctx, nb_ctx - 1 - i, n_blk + nb_ctx - 1 - i)
        for d in range(2):
            c = i if d == 0 else c_bwd
            base = pl.multiple_of(c * LRU_BLK, LRU_BLK)

            def step_rows(k):
                return pl.ds(base + k, SUBLANES, stride=n_step)

            x = jnp.concatenate([jnp.concatenate([xc_ref[j, step_rows(k), :] for j in range(n_lg)], axis=-1)
                                 for k in range(n_step)], axis=0)
            xb = x.astype(BF16)
            zs = [_dot(xb[:, j * LANES:(j + 1) * LANES], w_ref[d, j]) + b_ref[d, j] for j in range(n_lg)]
            z_r = jnp.concatenate([zj[:, :LANES] for zj in zs], axis=-1)
            z_i = jnp.concatenate([zj[:, LANES:] for zj in zs], axis=-1)
            log_a = _sigmoid(z_r) * decay_rate[d]
            a = jnp.exp(log_a)
            th = jnp.tanh(log_a)
            bb = jnp.sqrt(-2.0 * th / (1.0 - th)) * (_sigmoid(z_i) * x)
            order = list(range(n_step)) if d == 0 else list(range(n_step - 1, -1, -1))
            a_cum, b_cum = [None] * n_step, [None] * n_step
            for n, k in enumerate(order):
                sl = slice(k * SUBLANES, (k + 1) * SUBLANES)
                if n == 0:
                    a_cum[k], b_cum[k] = a[sl], bb[sl]
                else:
                    a_cum[k], b_cum[k] = a[sl] * a_cum[order[n - 1]], a[sl] * b_cum[order[n - 1]] + bb[sl]
            ga, gb = a_cum[order[-1]], b_cum[order[-1]]
            s = 1
            while s < SUBLANES:
                ok = (sub >= s) if d == 0 else (sub < SUBLANES - s)
                shift = s if d == 0 else SUBLANES - s
                gb = gb + ga * jnp.where(ok, pltpu.roll(gb, shift, 0), 0.0)
                ga = ga * jnp.where(ok, pltpu.roll(ga, shift, 0), 1.0)
                s *= 2
            h_in = jnp.broadcast_to(h_ref[d], (SUBLANES, wd))
            h_end = gb + ga * h_in
            if d == 0:
                h_grp = jnp.where(sub == 0, h_in, pltpu.roll(h_end, 1, 0))
                h_ref[d] = h_end[SUBLANES - 1:SUBLANES]
            else:
                h_grp = jnp.where(sub == SUBLANES - 1, h_in, pltpu.roll(h_end, SUBLANES - 1, 0))
                h_ref[d] = h_end[0:1]
            for k in range(n_step):
                h_k = b_cum[k] + a_cum[k] * h_grp
                for j in range(n_lg):
                    hs_ref[d, j, pl.ds(k, SUBLANES, stride=n_step), :] = h_k[:, j * LANES:(j + 1) * LANES]
            rows = pl.ds(base, LRU_BLK)
            o_ref[0, rows, :] += jnp.concatenate([hs_ref[d, j] for j in range(n_lg)], axis=-1)
        return carry

    lax.fori_loop(0, n_blk, block, 0)


def _lru(p, conv_b, conv_b_bias, w_bd, b_bd, lam, n_ctx):
    b, l, _ = p.shape
    return pl.pallas_call(
        functools.partial(_lru_kernel, n_ctx),
        grid=(b,),
        in_specs=[pl.BlockSpec((1, l, W_BR), lambda i: (i, 0, OFF_BX // W_BR)),
                  pl.BlockSpec((CONV_K, W_BR), lambda i: (0, 0)),
                  pl.BlockSpec((1, W_BR), lambda i: (0, 0)),
                  pl.BlockSpec((2, W_BR // LANES, LANES, 2 * LANES), lambda i: (0, 0, 0, 0)),
                  pl.BlockSpec((2, W_BR // LANES, 1, 2 * LANES), lambda i: (0, 0, 0, 0)),
                  pl.BlockSpec((2, 1, W_BR), lambda i: (0, 0, 0))],
        out_specs=pl.BlockSpec((1, l, W_BR), lambda i: (i, 0, 0)),
        out_shape=jax.ShapeDtypeStruct((b, l, W_BR), F32),
        scratch_shapes=[pltpu.VMEM((W_BR // LANES, l, LANES), F32), pltpu.VMEM((2, 1, W_BR), F32),
                        pltpu.VMEM((2, W_BR // LANES, LRU_BLK, LANES), F32)],
        compiler_params=_cparams("parallel"),
        name="rglru",
    )(p, conv_b, conv_b_bias, w_bd, b_bd, lam)


def _attn_kernel(n_ctx, rows, q_ref, k_ref, v_ref, bias_ref, o_ref):
    scale = DK ** -0.5
    n_blk = rows // Q_ROWS
    nq = Q_ROWS * GRID_W
    nk = K_ROWS * GRID_W
    kc = k_ref[0, :n_ctx, :]
    vc = v_ref[0, :n_ctx, :]

    blocks = [(slice(0, n_ctx), [(None, None)])]
    for j in range(n_blk):
        kb = min(max(Q_ROWS * j - WIN_R // 2, 0), rows - K_ROWS)
        var = 0 if j == 0 else (2 if j == n_blk - 1 else 1)
        k0 = n_ctx + kb * GRID_W
        blocks.append((slice(n_ctx + j * nq, n_ctx + (j + 1) * nq), [(slice(k0, k0 + nk), var), (None, None)]))

    def scores(block):
        q_rows, parts = block
        q = (q_ref[0, q_rows, :].astype(F32) * scale).astype(BF16)
        ss = []
        for k_rows, var in parts:
            s = _dot_nt(q, kc if k_rows is None else k_ref[0, k_rows, :])
            ss.append(s if var is None else s + bias_ref[0, var])
        return ss

    def finish(block, ss):
        q_rows, parts = block
        m = functools.reduce(jnp.maximum, [jnp.max(s, axis=-1, keepdims=True) for s in ss])
        ps = [jnp.exp(s - m) for s in ss]
        den = functools.reduce(jnp.add, [jnp.sum(p, axis=-1, keepdims=True) for p in ps])
        num = functools.reduce(jnp.add, [_dot(p.astype(BF16), vc if k_rows is None else v_ref[0, k_rows, :])
                                         for p, (k_rows, _) in zip(ps, parts)])
        o_ref[0, q_rows, :] = num / den

    pending = scores(blocks[0])
    for n, block in enumerate(blocks):
        upcoming = scores(blocks[n + 1]) if n + 1 < len(blocks) else None
        finish(block, pending)
        pending = upcoming


def _attention(p, bias, n_ctx):
    b, l, _ = p.shape
    rows = (l - n_ctx) // GRID_W
    cb = OFF_CQKV // DK

    def pspec(off):
        return pl.BlockSpec((1, l, DK), lambda h, i: (i, 0, cb + off + h))

    return pl.pallas_call(
        functools.partial(_attn_kernel, n_ctx, rows),
        grid=(H_C, b),
        in_specs=[pspec(0), pspec(H_C), pspec(2 * H_C),
                  pl.BlockSpec((1,) + bias.shape[1:], lambda h, i: (h, 0, 0, 0))],
        out_specs=pl.BlockSpec((1, l, DK), lambda h, i: (i, 0, h)),
        out_shape=jax.ShapeDtypeStruct((b, l, H_C * DK), F32),
        compiler_params=_cparams("parallel", "parallel"),
        name="natten",
    )(p, p, p, bias)


def _attention_bias(rpb, rows):
    n_blk = rows // Q_ROWS
    qc = np.arange(GRID_W)[:, None]
    kc = np.arange(GRID_W)[None, :]
    c0 = np.clip(qc - WIN_C // 2, 0, GRID_W - WIN_C)
    col_ok = (kc >= c0) & (kc < c0 + WIN_C)
    dc = np.clip(kc - qc, -(WIN_C - 1), WIN_C - 1) + WIN_C - 1
    onehot = (dc[None] == np.arange(2 * WIN_C - 1)[:, None, None]).astype(np.float32)
    cols = jnp.einsum("...rd,dqk->...rqk", rpb.astype(F32), onehot, precision=lax.Precision.HIGHEST)
    cols = jnp.where(col_ok, cols, NEG)
    masked = jnp.full(cols.shape[:-3] + (GRID_W, GRID_W), NEG, F32)
    variants = []
    for j in (0, 1, n_blk - 1):
        kb = min(max(Q_ROWS * j - WIN_R // 2, 0), rows - K_ROWS)
        q_blocks = []
        for a in range(Q_ROWS):
            qr = Q_ROWS * j + a
            r0 = min(max(qr - WIN_R // 2, 0), rows - WIN_R)
            k_blocks = []
            for b in range(K_ROWS):
                kr = kb + b
                k_blocks.append(cols[..., kr - qr + WIN_R - 1, :, :] if r0 <= kr < r0 + WIN_R else masked)
            q_blocks.append(jnp.concatenate(k_blocks, axis=-1))
        variants.append(jnp.concatenate(q_blocks, axis=-2))
    return jnp.stack(variants, axis=-3)


def _merge_kernel(ctx_tiles, tile_off, oa_ref, hb_ref, oc_ref, ga_ref, gb_ref, gc_ref, m0_ref, m1_ref, m2_ref,
                  on_ref, wb_ref, wo_ref, x_ref, gp_ref, gl_ref, gcx_ref, o_ref):
    oa = oa_ref[0]
    parts = []
    for h in range(H_A):
        xh = oa[:, h * DK:(h + 1) * DK]
        parts.append(xh * lax.rsqrt(jnp.mean(xh * xh, axis=-1, keepdims=True) + EPS) * on_ref[...])
    ya = jnp.concatenate(parts, axis=-1) * _silu(ga_ref[0].astype(F32))
    yb = hb_ref[0] * _silu(gb_ref[0].astype(F32))
    yc = oc_ref[0] * _silu(gc_ref[0].astype(F32))
    merged = (_sigmoid(m0_ref[0].astype(F32)) * _dot(ya.astype(BF16), wb_ref[0])
              + _sigmoid(m1_ref[0].astype(F32)) * _dot(yb.astype(BF16), wb_ref[1])
              + _sigmoid(m2_ref[0].astype(F32)) * _dot(yc.astype(BF16), wb_ref[2]))
    y = _dot(merged.astype(BF16), wo_ref[...])
    yn = y * lax.rsqrt(jnp.mean(y * y, axis=-1, keepdims=True) + EPS) * gp_ref[...]
    is_ctx = (pl.program_id(1) + tile_off) < ctx_tiles
    gate = jnp.where(is_ctx, gcx_ref[...], gl_ref[0])
    o_ref[0] = x_ref[0] + gate * yn


def _merge(oa, hb, oc, p, onorm, w_branch, w_out, xs, g_post, gate_lat, gate_ctx, n_ctx, lat_only):
    b, l, d = xs.shape
    tm = 256
    ctx_tiles = n_ctx // tm
    tile_off = ctx_tiles if lat_only else 0
    n_tiles = l // tm - tile_off

    def tok(width, col=0):
        return pl.BlockSpec((1, tm, width), lambda i, t: (i, t + tile_off, col))

    def const(shape):
        return pl.BlockSpec(shape, lambda i, t: (0,) * len(shape))

    return pl.pallas_call(
        functools.partial(_merge_kernel, ctx_tiles, tile_off),
        grid=(b, n_tiles),
        in_specs=[tok(W_BR), tok(W_BR), tok(W_BR),
                  tok(W_BR, OFF_AGATE // W_BR), tok(W_BR, OFF_BGATE // W_BR), tok(W_BR, OFF_CGATE // W_BR),
                  tok(d, OFF_MG // d), tok(d, OFF_MG // d + 1), tok(d, OFF_MG // d + 2),
                  const((1, DK)), const((3, W_BR, d)), const((d, d)),
                  tok(d), const((1, d)),
                  pl.BlockSpec((1, 1, d), lambda i, t: (i, 0, 0)), const((1, d))],
        out_specs=pl.BlockSpec((1, tm, d), lambda i, t: (i, t, 0)),
        out_shape=jax.ShapeDtypeStruct((b, n_tiles * tm, d), F32),
        compiler_params=_cparams("parallel", "parallel"),
        name="merge",
    )(oa, hb, oc, p, p, p, p, p, p, onorm, w_branch, w_out, xs, g_post, gate_lat, gate_ctx)


def _rope_tables(n_ctx, s_len):
    quarter = DK // 4
    pos = jnp.arange(s_len)
    inv_freq = ROPE_BASE ** (-jnp.arange(quarter, dtype=F32) / quarter)
    ang_r = (pos // GRID_W).astype(F32)[:, None] * inv_freq
    ang_c = (pos % GRID_W).astype(F32)[:, None] * inv_freq
    cos = jnp.concatenate([jnp.cos(ang_r)] * 2 + [jnp.cos(ang_c)] * 2, axis=-1)
    sin = jnp.concatenate([-jnp.sin(ang_r), jnp.sin(ang_r), -jnp.sin(ang_c), jnp.sin(ang_c)], axis=-1)
    cos = jnp.concatenate([jnp.ones((n_ctx, DK), F32), cos], axis=0)
    sin = jnp.concatenate([jnp.zeros((n_ctx, DK), F32), sin], axis=0)
    return cos, sin


def kernel(x, c, ctx, c_ctx, w_mod, b_mod, g_pre, g_post, w_in, conv_a, a_log, dt_bias, onorm_a, conv_b,
           conv_b_bias, lru_wa, lru_ba, lru_wx, lru_bx, lru_lam, rpb, w_branch, w_out):
    bsz, s_len, d = x.shape
    n_ctx = ctx.shape[1]
    depth = w_in.shape[0]
    l = n_ctx + s_len
    n_chunks = l // CHUNK
    rows = s_len // GRID_W
    assert d == 1024 and n_ctx % 256 == 0 and s_len % (Q_ROWS * GRID_W) == 0 and rows >= K_ROWS
    assert n_chunks % PREP_CHUNKS == 0 and (n_ctx // CHUNK) % PREP_CHUNKS == 0

    pad = (-(bsz + 1)) % SUBLANES
    cvec = jnp.concatenate([c, c_ctx[None], jnp.zeros((pad, d), F32)], axis=0)
    mod = _modulation(cvec, w_mod, b_mod)
    cos_t, sin_t = _rope_tables(n_ctx, s_len)
    attn_bias = _attention_bias(rpb, rows)

    sm0 = 4 * W_BR
    w_main = jnp.concatenate([w_in[:, :, :sm0], w_in[:, :, sm0 + 2 * N_CHAIN:]], axis=-1).astype(BF16)
    w_small = jnp.pad(w_in[:, :, sm0:sm0 + 2 * N_CHAIN], ((0, 0), (0, 0), (0, N_SMALL - 2 * N_CHAIN))).astype(BF16)
    lane_pad = ((0, 0), (N_CHAIN, N_SMALL - 2 * N_CHAIN))
    alog_l = jnp.pad(a_log.reshape(depth, N_CHAIN), lane_pad)
    dt_l = jnp.pad(dt_bias.reshape(depth, N_CHAIN), lane_pad)
    per_grp = LANES // (W_BR // NB_B)

    def lane_groups(w):
        w = w.reshape(depth, 2, NB_B // per_grp, per_grp, W_BR // NB_B, W_BR // NB_B)
        return jnp.einsum("ldgpab,pq->ldgpaqb", w, jnp.eye(per_grp, dtype=w.dtype)).reshape(
            depth, 2, NB_B // per_grp, LANES, LANES)

    lru_w = jnp.concatenate([lane_groups(lru_wa), lane_groups(lru_wx)], axis=-1).astype(BF16)
    lru_b = jnp.concatenate([lru_ba.reshape(depth, 2, W_BR // LANES, 1, LANES),
                             lru_bx.reshape(depth, 2, W_BR // LANES, 1, LANES)], axis=-1)
    w_br = w_branch.astype(BF16)
    w_o = w_out.astype(BF16)

    xs = jnp.concatenate([ctx, x], axis=1)
    for li in range(depth):
        last = li == depth - 1
        mod_lat = mod[li, :bsz, None, :]
        mod_ctx = mod[li, bsz:bsz + 1]
        p, psmall = _in_proj(xs, mod_lat, mod_ctx, g_pre[li][None], w_main[li], w_small[li], n_ctx)

        qn, kn, vn = _gdn_prep(p, conv_a[li], cos_t, sin_t, n_ctx)
        gsm = _gdn_gates(psmall, alog_l[li][None], dt_l[li][None])
        def rows_of(t):
            t = t.reshape(bsz, n_chunks, CHUNK, 2, H_A).transpose(0, 3, 1, 4, 2)
            return t.reshape(bsz, 2, n_chunks, H_A * CHUNK)

        oa = _gdn(qn, kn, vn, gsm, rows_of(gsm[:, :, N_CHAIN:2 * N_CHAIN]), n_ctx)

        hb = _lru(p, conv_b[li], conv_b_bias[li][None], lru_w[li], lru_b[li], lru_lam[li][:, None, :], n_ctx)

        oc = _attention(p, attn_bias[li], n_ctx)

        xs = _merge(oa, hb, oc, p, onorm_a[li][None], w_br[li], w_o[li], xs, g_post[li][None],
                    mod[li, :bsz, None, 2 * d:], mod[li, bsz:bsz + 1, 2 * d:], n_ctx, last)
    return xs
```

```python
import functools
import math

import numpy as np
import jax
import jax.numpy as jnp
from jax import lax
from jax.experimental import pallas as pl
from jax.experimental.pallas import tpu as pltpu

F32 = jnp.float32
BF16 = jnp.bfloat16

EPS = 1e-6
GRID_W = 64
CONV_K = 4
H_A = 4
DK = 128
CHUNK = 64
N_CHAIN = 2 * H_A
INV_BLK = 16
PREP_CHUNKS = 4
W_BR = 512
NB_B = 8
RGLRU_C = 8.0
H_C = 4
WIN_R = 8
WIN_C = 16
ROPE_BASE = 10000.0
Q_ROWS = 4
K_ROWS = 12
NEG = -1e30
LRU_BLK = 64
SUBLANES = 8
LANES = 128
VMEM_LIMIT = 56 * 1024 * 1024

OFF_AQKV, OFF_AGATE, OFF_BX, OFF_BGATE, OFF_CQKV, OFF_CGATE, OFF_MG = 0, 1536, 2048, 2560, 3072, 4608, 5120
N_MAIN = 8192
N_SMALL = 128


def _cparams(*sem):
    return pltpu.CompilerParams(dimension_semantics=sem, vmem_limit_bytes=VMEM_LIMIT)


def _dot(a, b):
    return jnp.dot(a, b, preferred_element_type=F32)


def _dot_nt(a, b):
    return lax.dot_general(a, b, (((1,), (1,)), ((), ())), preferred_element_type=F32)


def _sigmoid(x):
    return 0.5 * jnp.tanh(0.5 * x) + 0.5


def _silu(x):
    half = 0.5 * x
    return half * jnp.tanh(half) + half


def _softplus(x):
    return jnp.maximum(x, 0.0) + jnp.log1p(jnp.exp(-jnp.abs(x)))


def _mod_kernel(c_ref, w_ref, b_ref, o_ref):
    o_ref[0] = _dot(_silu(c_ref[...]).astype(BF16), w_ref[0].astype(BF16)) + b_ref[0]


def _modulation(cvec, w_mod, b_mod):
    depth, d, n = w_mod.shape
    rows = cvec.shape[0]
    tn = 768
    return pl.pallas_call(
        _mod_kernel,
        grid=(depth, n // tn),
        in_specs=[pl.BlockSpec((rows, d), lambda l, j: (0, 0)),
                  pl.BlockSpec((1, d, tn), lambda l, j: (l, 0, j)),
                  pl.BlockSpec((1, 1, tn), lambda l, j: (l, 0, j))],
        out_specs=pl.BlockSpec((1, rows, tn), lambda l, j: (l, 0, j)),
        out_shape=jax.ShapeDtypeStruct((depth, rows, n), F32),
        compiler_params=_cparams("parallel", "parallel"),
        name="modulation",
    )(cvec, w_mod, b_mod.reshape(depth, 1, n))


def _inproj_kernel(n_ctx, x_ref, ml_ref, mc_ref, g_ref, w_ref, ws_ref, o_ref, os_ref, h_ref):
    d = x_ref.shape[-1]

    @pl.when(pl.program_id(1) == 0)
    def _():
        g = g_ref[...]

        def normed(x, mod):
            y = x * lax.rsqrt(jnp.mean(x * x, axis=-1, keepdims=True) + EPS) * g
            return (y * (1.0 + mod[:, d:2 * d]) + mod[:, :d]).astype(BF16)

        h_ref[:n_ctx] = normed(x_ref[0, :n_ctx], mc_ref[...])
        h_ref[n_ctx:] = normed(x_ref[0, n_ctx:], ml_ref[0])
        os_ref[0] = _dot(h_ref[...], ws_ref[...])

    o_ref[0] = _dot(h_ref[...], w_ref[...]).astype(o_ref.dtype)


def _in_proj(xs, mod_lat, mod_ctx, g_pre, w_main, w_small, n_ctx):
    b, l, d = xs.shape
    tn = 1024
    return pl.pallas_call(
        functools.partial(_inproj_kernel, n_ctx),
        grid=(b, N_MAIN // tn),
        in_specs=[pl.BlockSpec((1, l, d), lambda i, j: (i, 0, 0)),
                  pl.BlockSpec((1, 1, 3 * d), lambda i, j: (i, 0, 0)),
                  pl.BlockSpec((1, 3 * d), lambda i, j: (0, 0)),
                  pl.BlockSpec((1, d), lambda i, j: (0, 0)),
                  pl.BlockSpec((d, tn), lambda i, j: (0, j)),
                  pl.BlockSpec((d, N_SMALL), lambda i, j: (0, 0))],
        out_specs=[pl.BlockSpec((1, l, tn), lambda i, j: (i, 0, j)),
                   pl.BlockSpec((1, l, N_SMALL), lambda i, j: (i, 0, 0))],
        out_shape=[jax.ShapeDtypeStruct((b, l, N_MAIN), BF16),
                   jax.ShapeDtypeStruct((b, l, N_SMALL), F32)],
        scratch_shapes=[pltpu.VMEM((l, d), BF16)],
        compiler_params=_cparams("parallel", "arbitrary"),
        name="in_proj",
    )(xs, mod_lat, mod_ctx, g_pre, w_main, w_small)


def _conv_centred(x, w, n_ctx):
    l = x.shape[0]
    t = lax.broadcasted_iota(jnp.int32, (l, 1), 0)
    tl = jnp.where(t < n_ctx, t, t - n_ctx)
    rem = jnp.where(t < n_ctx, n_ctx - 1 - t, l - 1 - t)
    acc = x * w[CONV_K // 2:CONV_K // 2 + 1]
    for k in range(CONV_K):
        off = k - CONV_K // 2
        if off == 0:
            continue
        shifted = pltpu.roll(x, (-off) % l, 0)
        ok = (tl >= -off) if off < 0 else (rem >= off)
        acc = acc + jnp.where(ok, shifted, 0.0) * w[k:k + 1]
    return acc


def _gdn_prep_kernel(n_ctx, pq_ref, pk_ref, pv_ref, wq_ref, wk_ref, wv_ref, cos_ref, sin_ref,
                     q_ref, k_ref, v_ref):
    lane = lax.broadcasted_iota(jnp.int32, (1, DK), 1)
    first = (lane % (DK // 2)) < (DK // 4)

    def qk_path(p_ref, w_ref, scale):
        x = _silu(_conv_centred(p_ref[0].astype(F32), w_ref[...], n_ctx))
        x = x * lax.rsqrt(jnp.sum(x * x, axis=-1, keepdims=True) + EPS)
        partner = jnp.where(first, pltpu.roll(x, DK - DK // 4, 1), pltpu.roll(x, DK // 4, 1))
        x = x * cos_ref[...] + partner * sin_ref[...]
        if scale != 1.0:
            x = x * scale
        return x.astype(BF16)

    q_ref[0] = qk_path(pq_ref, wq_ref, DK ** -0.5)
    k_ref[0] = qk_path(pk_ref, wk_ref, 1.0)
    v_ref[0] = _silu(_conv_centred(pv_ref[0].astype(F32), wv_ref[...], n_ctx)).astype(BF16)


def _gdn_prep(p, conv_a, cos_t, sin_t, n_ctx):
    b, l, _ = p.shape
    cb = OFF_AQKV // DK

    def pspec(off):
        return pl.BlockSpec((1, l, DK), lambda i, h: (i, 0, cb + off + h))

    def wspec(off):
        return pl.BlockSpec((CONV_K, DK), lambda i, h: (0, off + h))

    tab = pl.BlockSpec((l, DK), lambda i, h: (0, 0))
    out = pl.BlockSpec((1, l, DK), lambda i, h: (i, 0, h))
    shp = jax.ShapeDtypeStruct((b, l, H_A * DK), BF16)
    return pl.pallas_call(
        functools.partial(_gdn_prep_kernel, n_ctx),
        grid=(b, H_A),
        in_specs=[pspec(0), pspec(H_A), pspec(2 * H_A), wspec(0), wspec(H_A), wspec(2 * H_A), tab, tab],
        out_specs=[out, out, out],
        out_shape=[shp, shp, shp],
        compiler_params=_cparams("parallel", "parallel"),
        name="gdn_prep",
    )(p, p, p, conv_a, conv_a, conv_a, cos_t, sin_t)


def _gdn_gate_kernel(ps_ref, alog_ref, dt_ref, o_ref):
    x = ps_ref[0]
    l = x.shape[0]
    lane = lax.broadcasted_iota(jnp.int32, (1, N_SMALL), 1)
    g = -jnp.exp(alog_ref[...]) * _softplus(x + dt_ref[...])
    pos = lax.broadcasted_iota(jnp.int32, (l, 1), 0) % CHUNK
    fwd, bwd = g, g
    s = 1
    while s < CHUNK:
        fwd = fwd + jnp.where(pos >= s, pltpu.roll(fwd, s, 0), 0.0)
        bwd = bwd + jnp.where(pos < CHUNK - s, pltpu.roll(bwd, l - s, 0), 0.0)
        s *= 2
    gc = jnp.where(lane < 2 * H_A + H_A, fwd, bwd)
    o_ref[0] = jnp.where(lane < 2 * H_A, _sigmoid(x), gc)


def _gdn_gates(psmall, alog_l, dt_l):
    b, l, n = psmall.shape
    vec = pl.BlockSpec((1, n), lambda i: (0, 0))
    return pl.pallas_call(
        _gdn_gate_kernel,
        grid=(b,),
        in_specs=[pl.BlockSpec((1, l, n), lambda i: (i, 0, 0)), vec, vec],
        out_specs=pl.BlockSpec((1, l, n), lambda i: (i, 0, 0)),
        out_shape=jax.ShapeDtypeStruct((b, l, n), F32),
        compiler_params=_cparams("parallel"),
        name="gdn_gates",
    )(psmall, alog_l, dt_l)


def _head_block_mask(shape, row_blk, col_blk):
    r = lax.broadcasted_iota(jnp.int32, shape, 0) // row_blk
    c = lax.broadcasted_iota(jnp.int32, shape, 1) // col_blk
    return r == c


def _block_diag_tile(x, reps, mask):
    return jnp.where(mask, jnp.concatenate([x] * reps, axis=0), jnp.zeros((), x.dtype))


def _dot_tn(a, b):
    return lax.dot_general(a, b, (((0,), (0,)), ((), ())), preferred_element_type=F32)


def _gdn_kernel(n_ctx, q_ref, k_ref, v_ref, gs_ref, gr_ref, o_ref, wv_ref, wk_ref, qk_ref, s_ref):
    l = q_ref.shape[1]
    n_chunks = l // CHUNK
    nc_ctx = n_ctx // CHUNK
    n_groups = n_chunks // PREP_CHUNKS
    wide = H_A * CHUNK
    n_levels = INV_BLK.bit_length() - 1
    ii = lax.broadcasted_iota(jnp.int32, (CHUNK, wide), 0)
    jw = lax.broadcasted_iota(jnp.int32, (CHUNK, wide), 1)
    jj, hw = jw % CHUNK, jw // CHUNK
    eye_w = jnp.where(ii == jj, 1.0, 0.0)
    m_kbd = _head_block_mask((wide, H_A * DK), CHUNK, DK)
    m_pbd = _head_block_mask((wide, wide), CHUNK, CHUNK)
    m_ubd = _head_block_mask((2 * CHUNK, 2 * DK), CHUNK, DK)
    m_sbd = _head_block_mask((2 * DK, 2 * DK), DK, DK)

    def chunk_rows(c):
        return pl.ds(pl.multiple_of(c * CHUNK, CHUNK), CHUNK)

    def step_chunk(i, d):
        return i if d == 0 else jnp.where(i < nc_ctx, nc_ctx - 1 - i, n_chunks + nc_ctx - 1 - i)

    def heads_to_rows(x):
        return jnp.concatenate([x[:, h * DK:(h + 1) * DK] for h in range(H_A)], axis=0)

    def splat_heads(cols):
        return jnp.concatenate([jnp.broadcast_to(c, (CHUNK, DK)) for c in cols], axis=-1)

    def bd(x):
        return _block_diag_tile(x, H_A, m_pbd)

    def prepare_stages(g):
        streams = []

        def load():
            for ci in range(PREP_CHUNKS):
                for d in range(2):
                    c = step_chunk(g * PREP_CHUNKS + ci, d)
                    rows = chunk_rows(c)
                    q_all, k_all, v_all = q_ref[0, rows, :], k_ref[0, rows, :], v_ref[0, rows, :]
                    gs = gs_ref[0, rows, :]
                    lane0 = N_CHAIN + d * H_A
                    beta = splat_heads([gs[:, d * H_A + h:d * H_A + h + 1] for h in range(H_A)])
                    e_gc = splat_heads([jnp.exp(gs[:, lane0 + h:lane0 + h + 1]) for h in range(H_A)])
                    kb = k_all.astype(F32) * beta
                    prod = _dot_nt(jnp.concatenate([kb.astype(BF16), q_all], axis=0),
                                   _block_diag_tile(k_all, H_A, m_kbd))
                    rhs = jnp.concatenate([heads_to_rows((v_all.astype(F32) * beta).astype(BF16)),
                                           heads_to_rows((kb * e_gc).astype(BF16))], axis=-1)
                    gc_w = jnp.broadcast_to(gs[:, lane0 + H_A - 1:lane0 + H_A], (CHUNK, wide))
                    for h in range(H_A - 2, -1, -1):
                        gc_w = jnp.where(hw == h, jnp.broadcast_to(gs[:, lane0 + h:lane0 + h + 1], (CHUNK, wide)),
                                         gc_w)
                    g_row = gr_ref[0, d, pl.ds(c, 1), :]
                    incl = (ii >= jj) if d == 0 else (ii <= jj)
                    strict = (ii > jj) if d == 0 else (ii < jj)
                    decay = jnp.where(incl, jnp.exp(jnp.where(incl, gc_w - g_row, 0.0)), 0.0)
                    a = jnp.where(strict, prod[:CHUNK] * decay, 0.0)
                    p = jnp.where(ii // INV_BLK == jj // INV_BLK, -a, 0.0)
                    streams.append(dict(d=d, rows=rows, a=a, p=p, t=eye_w + p, rhs=rhs,
                                        qk=(prod[CHUNK:] * decay).astype(BF16)))

        def square():
            for st in streams:
                pb = st["p"].astype(BF16)
                st["p"] = _dot(pb, bd(pb))

        def square_and_extend():
            for st in streams:
                pb = st["p"].astype(BF16)
                r = _dot(jnp.concatenate([pb, st["t"].astype(BF16)], axis=0), bd(pb))
                st["p"], st["t"] = r[:CHUNK], st["t"] + r[CHUNK:]

        def extend():
            for st in streams:
                st["t"] = st["t"] + _dot(st["t"].astype(BF16), bd(st["p"].astype(BF16)))

        def join_first(blk):
            def run():
                joins = (ii // (2 * blk) == jj // (2 * blk)) & (ii // blk != jj // blk)
                for st in streams:
                    st["tb"] = st["t"].astype(BF16)
                    st["w"] = _dot(jnp.where(joins, st["a"], 0.0).astype(BF16), bd(st["tb"]))
            return run

        def join_second():
            for st in streams:
                st["t"] = st["t"] - _dot(st["tb"], bd(st["w"].astype(BF16)))

        def solve():
            for st in streams:
                st["sol"] = _dot(bd(st["t"].astype(BF16)), st["rhs"])

        def store():
            for st in streams:
                d, rows = st["d"], st["rows"]
                qk_ref[d, rows, :] = st["qk"]
                for h in range(H_A):
                    sol_h = st["sol"][h * CHUNK:(h + 1) * CHUNK]
                    wv_ref[d, rows, h * DK:(h + 1) * DK] = sol_h[:, :DK].astype(BF16)
                    wk_ref[d, rows, h * DK:(h + 1) * DK] = sol_h[:, DK:].astype(BF16)

        stages = [load, square] + [square_and_extend] * (n_levels - 2) + [extend]
        blk = INV_BLK
        while blk < CHUNK:
            stages += [join_first(blk), join_second]
            blk *= 2
        return stages + [solve], store

    def recur_stages(g):
        stages = []
        for ci in range(PREP_CHUNKS):
            chains = []

            def start(ci=ci, chains=chains):
                for d in range(2):
                    c = step_chunk(g * PREP_CHUNKS + ci, d)
                    rows = chunk_rows(c)
                    gs = gs_ref[0, rows, :]
                    g_row = gr_ref[0, d, pl.ds(c, 1), :]
                    last = CHUNK - 1 if d == 0 else 0
                    for pr in range(H_A // 2):
                        hs = (2 * pr, 2 * pr + 1)
                        cols = slice(2 * pr * DK, (2 * pr + 2) * DK)
                        gcol = [gs[:, N_CHAIN + d * H_A + h:N_CHAIN + d * H_A + h + 1] for h in hs]
                        glast = [g_row[:, h * CHUNK + last:h * CHUNK + last + 1] for h in hs]
                        e_q = jnp.concatenate([jnp.broadcast_to(jnp.exp(gc), (CHUNK, DK)) for gc in gcol], axis=-1)
                        e_k = jnp.concatenate([jnp.broadcast_to(jnp.exp(gl - gc), (CHUNK, DK))
                                               for gc, gl in zip(gcol, glast)], axis=-1)
                        e_s = jnp.concatenate([jnp.broadcast_to(jnp.exp(gl), (1, DK)) for gl in glast], axis=-1)
                        q_dec = (q_ref[0, rows, cols].astype(F32) * e_q).astype(BF16)
                        k_tail = (k_ref[0, rows, cols].astype(F32) * e_k).astype(BF16)
                        s = s_ref[d, pr]
                        r = _dot(jnp.concatenate([wk_ref[d, rows, cols], q_dec], axis=0), s.astype(BF16))
                        chains.append(dict(d=d, pr=pr, rows=rows, cols=cols, e_s=e_s, k_tail=k_tail, s=s, r=r))

            def middle(chains=chains):
                for ch in chains:
                    d, pr, rows, cols = ch["d"], ch["pr"], ch["rows"], ch["cols"]
                    u_b = (wv_ref[d, rows, cols].astype(F32) - ch["r"][:CHUNK]).astype(BF16)
                    ch["ku"] = _dot_tn(ch["k_tail"], u_b)
                    ch["o"] = ch["r"][CHUNK:] + _dot(qk_ref[d, rows, 2 * pr * CHUNK:(2 * pr + 2) * CHUNK],
                                                     _block_diag_tile(u_b, 2, m_ubd))

            def finish(chains=chains):
                for ch in chains:
                    s_ref[ch["d"], ch["pr"]] = ch["s"] * ch["e_s"] + jnp.where(m_sbd, ch["ku"], 0.0)
                    o_ref[0, ch["rows"], ch["cols"]] += ch["o"]

            stages += [start, middle, finish]
        return stages

    def run_interleaved(first, second):
        for n in range(max(len(first), len(second))):
            if n < len(first):
                first[n]()
            if n < len(second):
                second[n]()

    s_ref[...] = jnp.zeros_like(s_ref)
    o_ref[...] = jnp.zeros_like(o_ref)

    stages, store = prepare_stages(0)
    run_interleaved(stages, [])
    store()

    def body(g, carry):
        stages, store = prepare_stages(g + 1)
        run_interleaved(stages, recur_stages(g))
        store()
        return carry

    lax.fori_loop(0, n_groups - 1, body, 0)
    run_interleaved([], recur_stages(n_groups - 1))


def _gdn(qn, kn, vn, gsm, g_rows, n_ctx):
    b, l, w = qn.shape
    n_chunks = l // CHUNK
    wide = H_A * CHUNK
    seq = pl.BlockSpec((1, l, w), lambda i: (i, 0, 0))
    return pl.pallas_call(
        functools.partial(_gdn_kernel, n_ctx),
        grid=(b,),
        in_specs=[seq, seq, seq, pl.BlockSpec((1, l, N_SMALL), lambda i: (i, 0, 0)),
                  pl.BlockSpec((1, 2, n_chunks, wide), lambda i: (i, 0, 0, 0))],
        out_specs=pl.BlockSpec((1, l, w), lambda i: (i, 0, 0)),
        out_shape=jax.ShapeDtypeStruct((b, l, w), F32),
        scratch_shapes=[pltpu.VMEM((2, l, w), BF16),
                        pltpu.VMEM((2, l, w), BF16),
                        pltpu.VMEM((2, l, wide), BF16),
                        pltpu.VMEM((2, H_A // 2, 2 * DK, 2 * DK), F32)],
        compiler_params=_cparams("parallel"),
        name="gdn",
    )(qn, kn, vn, gsm, g_rows)


def _lru_kernel(n_ctx, x_ref, cw_ref, cb_ref, w_ref, b_ref, lam_ref, o_ref, xc_ref, h_ref, hs_ref):
    l = x_ref.shape[1]
    n_blk = l // LRU_BLK
    nb_ctx = n_ctx // LRU_BLK
    wd = x_ref.shape[2]
    xc = _conv_centred(x_ref[0].astype(F32), cw_ref[...], n_ctx) + cb_ref[...]
    n_lg = wd // LANES
    for j in range(n_lg):
        xc_ref[j] = xc[:, j * LANES:(j + 1) * LANES]
    h_ref[...] = jnp.zeros_like(h_ref)
    o_ref[...] = jnp.zeros_like(o_ref)
    sub = lax.broadcasted_iota(jnp.int32, (SUBLANES, 1), 0)
    n_step = LRU_BLK // SUBLANES
    decay_rate = [(-RGLRU_C) * _softplus(-lam_ref[d]) for d in range(2)]

    def block(i, carry):
        c_bwd = jnp.where(i < nb_ctx, nb_ctx - 1 - i, n_blk + nb_ctx - 1 - i)
        for d in range(2):
            c = i if d == 0 else c_bwd
            base = pl.multiple_of(c * LRU_BLK, LRU_BLK)

            def step_rows(k):
                return pl.ds(base + k, SUBLANES, stride=n_step)

            x = jnp.concatenate([jnp.concatenate([xc_ref[j, step_rows(k), :] for j in range(n_lg)], axis=-1)
                                 for k in range(n_step)], axis=0)
            xb = x.astype(BF16)
            zs = [_dot(xb[:, j * LANES:(j + 1) * LANES], w_ref[d, j]) + b_ref[d, j] for j in range(n_lg)]
            z_r = jnp.concatenate([zj[:, :LANES] for zj in zs], axis=-1)
            z_i = jnp.concatenate([zj[:, LANES:] for zj in zs], axis=-1)
            log_a = _sigmoid(z_r) * decay_rate[d]
            a = jnp.exp(log_a)
            th = jnp.tanh(log_a)
            bb = jnp.sqrt(-2.0 * th / (1.0 - th)) * (_sigmoid(z_i) * x)
            order = list(range(n_step)) if d == 0 else list(range(n_step - 1, -1, -1))
            a_cum, b_cum = [None] * n_step, [None] * n_step
            for n, k in enumerate(order):
                sl = slice(k * SUBLANES, (k + 1) * SUBLANES)
                if n == 0:
                    a_cum[k], b_cum[k] = a[sl], bb[sl]
                else:
                    a_cum[k], b_cum[k] = a[sl] * a_cum[order[n - 1]], a[sl] * b_cum[order[n - 1]] + bb[sl]
            ga, gb = a_cum[order[-1]], b_cum[order[-1]]
            s = 1
            while s < SUBLANES:
                ok = (sub >= s) if d == 0 else (sub < SUBLANES - s)
                shift = s if d == 0 else SUBLANES - s
                gb = gb + ga * jnp.where(ok, pltpu.roll(gb, shift, 0), 0.0)
                ga = ga * jnp.where(ok, pltpu.roll(ga, shift, 0), 1.0)
                s *= 2
            h_in = jnp.broadcast_to(h_ref[d], (SUBLANES, wd))
            h_end = gb + ga * h_in
            if d == 0:
                h_grp = jnp.where(sub == 0, h_in, pltpu.roll(h_end, 1, 0))
                h_ref[d] = h_end[SUBLANES - 1:SUBLANES]
            else:
                h_grp = jnp.where(sub == SUBLANES - 1, h_in, pltpu.roll(h_end, SUBLANES - 1, 0))
                h_ref[d] = h_end[0:1]
            for k in range(n_step):
                h_k = b_cum[k] + a_cum[k] * h_grp
                for j in range(n_lg):
                    hs_ref[d, j, pl.ds(k, SUBLANES, stride=n_step), :] = h_k[:, j * LANES:(j + 1) * LANES]
            rows = pl.ds(base, LRU_BLK)
            o_ref[0, rows, :] += jnp.concatenate([hs_ref[d, j] for j in range(n_lg)], axis=-1)
        return carry

    lax.fori_loop(0, n_blk, block, 0, unroll=4)


def _lru(p, conv_b, conv_b_bias, w_bd, b_bd, lam, n_ctx):
    b, l, _ = p.shape
    return pl.pallas_call(
        functools.partial(_lru_kernel, n_ctx),
        grid=(b,),
        in_specs=[pl.BlockSpec((1, l, W_BR), lambda i: (i, 0, OFF_BX // W_BR)),
                  pl.BlockSpec((CONV_K, W_BR), lambda i: (0, 0)),
                  pl.BlockSpec((1, W_BR), lambda i: (0, 0)),
                  pl.BlockSpec((2, W_BR // LANES, LANES, 2 * LANES), lambda i: (0, 0, 0, 0)),
                  pl.BlockSpec((2, W_BR // LANES, 1, 2 * LANES), lambda i: (0, 0, 0, 0)),
                  pl.BlockSpec((2, 1, W_BR), lambda i: (0, 0, 0))],
        out_specs=pl.BlockSpec((1, l, W_BR), lambda i: (i, 0, 0)),
        out_shape=jax.ShapeDtypeStruct((b, l, W_BR), F32),
        scratch_shapes=[pltpu.VMEM((W_BR // LANES, l, LANES), F32), pltpu.VMEM((2, 1, W_BR), F32),
                        pltpu.VMEM((2, W_BR // LANES, LRU_BLK, LANES), F32)],
        compiler_params=_cparams("parallel"),
        name="rglru",
    )(p, conv_b, conv_b_bias, w_bd, b_bd, lam)


def _attn_kernel(n_ctx, rows, q_ref, k_ref, v_ref, bias_ref, o_ref):
    scale = DK ** -0.5
    n_blk = rows // Q_ROWS
    nq = Q_ROWS * GRID_W
    nk = K_ROWS * GRID_W
    kc = k_ref[0, :n_ctx, :]
    vc = v_ref[0, :n_ctx, :]

    blocks = [(slice(0, n_ctx), [(None, None)])]
    for j in range(n_blk):
        kb = min(max(Q_ROWS * j - WIN_R // 2, 0), rows - K_ROWS)
        var = 0 if j == 0 else (2 if j == n_blk - 1 else 1)
        k0 = n_ctx + kb * GRID_W
        blocks.append((slice(n_ctx + j * nq, n_ctx + (j + 1) * nq), [(slice(k0, k0 + nk), var), (None, None)]))

    def scores(block):
        q_rows, parts = block
        q = (q_ref[0, q_rows, :].astype(F32) * scale).astype(BF16)
        ss = []
        for k_rows, var in parts:
            s = _dot_nt(q, kc if k_rows is None else k_ref[0, k_rows, :])
            ss.append(s if var is None else s + bias_ref[0, var])
        return ss

    def finish(block, ss):
        q_rows, parts = block
        m = functools.reduce(jnp.maximum, [jnp.max(s, axis=-1, keepdims=True) for s in ss])
        ps = [jnp.exp(s - m) for s in ss]
        den = functools.reduce(jnp.add, [jnp.sum(p, axis=-1, keepdims=True) for p in ps])
        num = functools.reduce(jnp.add, [_dot(p.astype(BF16), vc if k_rows is None else v_ref[0, k_rows, :])
                                         for p, (k_rows, _) in zip(ps, parts)])
        o_ref[0, q_rows, :] = num / den

    pending = scores(blocks[0])
    for n, block in enumerate(blocks):
        upcoming = scores(blocks[n + 1]) if n + 1 < len(blocks) else None
        finish(block, pending)
        pending = upcoming


def _attention(p, bias, n_ctx):
    b, l, _ = p.shape
    rows = (l - n_ctx) // GRID_W
    cb = OFF_CQKV // DK

    def pspec(off):
        return pl.BlockSpec((1, l, DK), lambda h, i: (i, 0, cb + off + h))

    return pl.pallas_call(
        functools.partial(_attn_kernel, n_ctx, rows),
        grid=(H_C, b),
        in_specs=[pspec(0), pspec(H_C), pspec(2 * H_C),
                  pl.BlockSpec((1,) + bias.shape[1:], lambda h, i: (h, 0, 0, 0))],
        out_specs=pl.BlockSpec((1, l, DK), lambda h, i: (i, 0, h)),
        out_shape=jax.ShapeDtypeStruct((b, l, H_C * DK), F32),
        compiler_params=_cparams("parallel", "parallel"),
        name="natten",
    )(p, p, p, bias)


def _attention_bias(rpb, rows):
    n_blk = rows // Q_ROWS
    qc = np.arange(GRID_W)[:, None]
    kc = np.arange(GRID_W)[None, :]
    c0 = np.clip(qc - WIN_C // 2, 0, GRID_W - WIN_C)
    col_ok = (kc >= c0) & (kc < c0 + WIN_C)
    dc = np.clip(kc - qc, -(WIN_C - 1), WIN_C - 1) + WIN_C - 1
    onehot = (dc[None] == np.arange(2 * WIN_C - 1)[:, None, None]).astype(np.float32)
    cols = jnp.einsum("...rd,dqk->...rqk", rpb.astype(F32), onehot, precision=lax.Precision.HIGHEST)
    cols = jnp.where(col_ok, cols, NEG)
    masked = jnp.full(cols.shape[:-3] + (GRID_W, GRID_W), NEG, F32)
    variants = []
    for j in (0, 1, n_blk - 1):
        kb = min(max(Q_ROWS * j - WIN_R // 2, 0), rows - K_ROWS)
        q_blocks = []
        for a in range(Q_ROWS):
            qr = Q_ROWS * j + a
            r0 = min(max(qr - WIN_R // 2, 0), rows - WIN_R)
            k_blocks = []
            for b in range(K_ROWS):
                kr = kb + b
                k_blocks.append(cols[..., kr - qr + WIN_R - 1, :, :] if r0 <= kr < r0 + WIN_R else masked)
            q_blocks.append(jnp.concatenate(k_blocks, axis=-1))
        variants.append(jnp.concatenate(q_blocks, axis=-2))
    return jnp.stack(variants, axis=-3)


def _merge_kernel(ctx_tiles, tile_off, oa_ref, hb_ref, oc_ref, ga_ref, gb_ref, gc_ref, m0_ref, m1_ref, m2_ref,
                  on_ref, wb_ref, wo_ref, x_ref, gp_ref, gl_ref, gcx_ref, o_ref):
    oa = oa_ref[0]
    parts = []
    for h in range(H_A):
        xh = oa[:, h * DK:(h + 1) * DK]
        parts.append(xh * lax.rsqrt(jnp.mean(xh * xh, axis=-1, keepdims=True) + EPS) * on_ref[...])
    ya = jnp.concatenate(parts, axis=-1) * _silu(ga_ref[0].astype(F32))
    yb = hb_ref[0] * _silu(gb_ref[0].astype(F32))
    yc = oc_ref[0] * _silu(gc_ref[0].astype(F32))
    merged = (_sigmoid(m0_ref[0].astype(F32)) * _dot(ya.astype(BF16), wb_ref[0])
              + _sigmoid(m1_ref[0].astype(F32)) * _dot(yb.astype(BF16), wb_ref[1])
              + _sigmoid(m2_ref[0].astype(F32)) * _dot(yc.astype(BF16), wb_ref[2]))
    y = _dot(merged.astype(BF16), wo_ref[...])
    yn = y * lax.rsqrt(jnp.mean(y * y, axis=-1, keepdims=True) + EPS) * gp_ref[...]
    is_ctx = (pl.program_id(1) + tile_off) < ctx_tiles
    gate = jnp.where(is_ctx, gcx_ref[...], gl_ref[0])
    o_ref[0] = x_ref[0] + gate * yn


def _merge(oa, hb, oc, p, onorm, w_branch, w_out, xs, g_post, gate_lat, gate_ctx, n_ctx, lat_only):
    b, l, d = xs.shape
    tm = 256
    ctx_tiles = n_ctx // tm
    tile_off = ctx_tiles if lat_only else 0
    n_tiles = l // tm - tile_off

    def tok(width, col=0):
        return pl.BlockSpec((1, tm, width), lambda i, t: (i, t + tile_off, col))

    def const(shape):
        return pl.BlockSpec(shape, lambda i, t: (0,) * len(shape))

    return pl.pallas_call(
        functools.partial(_merge_kernel, ctx_tiles, tile_off),
        grid=(b, n_tiles),
        in_specs=[tok(W_BR), tok(W_BR), tok(W_BR),
                  tok(W_BR, OFF_AGATE // W_BR), tok(W_BR, OFF_BGATE // W_BR), tok(W_BR, OFF_CGATE // W_BR),
                  tok(d, OFF_MG // d), tok(d, OFF_MG // d + 1), tok(d, OFF_MG // d + 2),
                  const((1, DK)), const((3, W_BR, d)), const((d, d)),
                  tok(d), const((1, d)),
                  pl.BlockSpec((1, 1, d), lambda i, t: (i, 0, 0)), const((1, d))],
        out_specs=pl.BlockSpec((1, tm, d), lambda i, t: (i, t, 0)),
        out_shape=jax.ShapeDtypeStruct((b, n_tiles * tm, d), F32),
        compiler_params=_cparams("parallel", "parallel"),
        name="merge",
    )(oa, hb, oc, p, p, p, p, p, p, onorm, w_branch, w_out, xs, g_post, gate_lat, gate_ctx)


def _rope_tables(n_ctx, s_len):
    quarter = DK // 4
    pos = jnp.arange(s_len)
    inv_freq = ROPE_BASE ** (-jnp.arange(quarter, dtype=F32) / quarter)
    ang_r = (pos // GRID_W).astype(F32)[:, None] * inv_freq
    ang_c = (pos % GRID_W).astype(F32)[:, None] * inv_freq
    cos = jnp.concatenate([jnp.cos(ang_r)] * 2 + [jnp.cos(ang_c)] * 2, axis=-1)
    sin = jnp.concatenate([-jnp.sin(ang_r), jnp.sin(ang_r), -jnp.sin(ang_c), jnp.sin(ang_c)], axis=-1)
    cos = jnp.concatenate([jnp.ones((n_ctx, DK), F32), cos], axis=0)
    sin = jnp.concatenate([jnp.zeros((n_ctx, DK), F32), sin], axis=0)
    return cos, sin


def kernel(x, c, ctx, c_ctx, w_mod, b_mod, g_pre, g_post, w_in, conv_a, a_log, dt_bias, onorm_a, conv_b,
           conv_b_bias, lru_wa, lru_ba, lru_wx, lru_bx, lru_lam, rpb, w_branch, w_out):
    bsz, s_len, d = x.shape
    n_ctx = ctx.shape[1]
    depth = w_in.shape[0]
    l = n_ctx + s_len
    n_chunks = l // CHUNK
    rows = s_len // GRID_W
    assert d == 1024 and n_ctx % 256 == 0 and s_len % (Q_ROWS * GRID_W) == 0 and rows >= K_ROWS
    assert n_chunks % PREP_CHUNKS == 0 and (n_ctx // CHUNK) % PREP_CHUNKS == 0

    pad = (-(bsz + 1)) % SUBLANES
    cvec = jnp.concatenate([c, c_ctx[None], jnp.zeros((pad, d), F32)], axis=0)
    mod = _modulation(cvec, w_mod, b_mod)
    cos_t, sin_t = _rope_tables(n_ctx, s_len)
    attn_bias = _attention_bias(rpb, rows)

    sm0 = 4 * W_BR
    w_main = jnp.concatenate([w_in[:, :, :sm0], w_in[:, :, sm0 + 2 * N_CHAIN:]], axis=-1).astype(BF16)
    w_small = jnp.pad(w_in[:, :, sm0:sm0 + 2 * N_CHAIN], ((0, 0), (0, 0), (0, N_SMALL - 2 * N_CHAIN))).astype(BF16)
    lane_pad = ((0, 0), (N_CHAIN, N_SMALL - 2 * N_CHAIN))
    alog_l = jnp.pad(a_log.reshape(depth, N_CHAIN), lane_pad)
    dt_l = jnp.pad(dt_bias.reshape(depth, N_CHAIN), lane_pad)
    per_grp = LANES // (W_BR // NB_B)

    def lane_groups(w):
        w = w.reshape(depth, 2, NB_B // per_grp, per_grp, W_BR // NB_B, W_BR // NB_B)
        return jnp.einsum("ldgpab,pq->ldgpaqb", w, jnp.eye(per_grp, dtype=w.dtype)).reshape(
            depth, 2, NB_B // per_grp, LANES, LANES)

    lru_w = jnp.concatenate([lane_groups(lru_wa), lane_groups(lru_wx)], axis=-1).astype(BF16)
    lru_b = jnp.concatenate([lru_ba.reshape(depth, 2, W_BR // LANES, 1, LANES),
                             lru_bx.reshape(depth, 2, W_BR // LANES, 1, LANES)], axis=-1)
    w_br = w_branch.astype(BF16)
    w_o = w_out.astype(BF16)

    xs = jnp.concatenate([ctx, x], axis=1)
    for li in range(depth):
        last = li == depth - 1
        mod_lat = mod[li, :bsz, None, :]
        mod_ctx = mod[li, bsz:bsz + 1]
        p, psmall = _in_proj(xs, mod_lat, mod_ctx, g_pre[li][None], w_main[li], w_small[li], n_ctx)

        qn, kn, vn = _gdn_prep(p, conv_a[li], cos_t, sin_t, n_ctx)
        gsm = _gdn_gates(psmall, alog_l[li][None], dt_l[li][None])
        def rows_of(t):
            t = t.reshape(bsz, n_chunks, CHUNK, 2, H_A).transpose(0, 3, 1, 4, 2)
            return t.reshape(bsz, 2, n_chunks, H_A * CHUNK)

        oa = _gdn(qn, kn, vn, gsm, rows_of(gsm[:, :, N_CHAIN:2 * N_CHAIN]), n_ctx)

        hb = _lru(p, conv_b[li], conv_b_bias[li][None], lru_w[li], lru_b[li], lru_lam[li][:, None, :], n_ctx)

        oc = _attention(p, attn_bias[li], n_ctx)

        xs = _merge(oa, hb, oc, p, onorm_a[li][None], w_br[li], w_o[li], xs, g_post[li][None],
                    mod[li, :bsz, None, 2 * d:], mod[li, bsz:bsz + 1, 2 * d:], n_ctx, last)
    return xs
```
